```python
import math
import jax, jax.numpy as jnp
from jax import lax
import numpy as np

D_MODEL = 1024
BATCH = 2
SEQ = 16384
DEPTH = 4
DEC_BATCH = 16
DEC_SEQ = 64
PAST_LEN = 4096

CHUNK = 64
N_META = 16
EPS = 1e-6
ATT_HEADS = 4
ATT_DH = 64
ATT_VD = 2 * ATT_DH
ATT_WIDTH = ATT_HEADS * ATT_VD
Q_BLOCK = 128
LRU_WIDTH = 256
LRU_BLOCKS = 8
LRU_BW = LRU_WIDTH // LRU_BLOCKS
LRU_CONV = 4
LRU_C = 8.0
GLA_HEADS = 4
GLA_DK = 32
GLA_DV = 64
GLA_KW = GLA_HEADS * GLA_DK
GLA_VW = GLA_HEADS * GLA_DV
GLA_RANK = 16
GLA_TAU = 16.0
GLA_BLOCK = 64
MIX_WIDTH = ATT_WIDTH + LRU_WIDTH + GLA_VW
IN_SIZES = (ATT_HEADS * 2 * ATT_DH, ATT_HEADS * 2 * ATT_DH, ATT_WIDTH,
            LRU_WIDTH, LRU_WIDTH,
            GLA_KW, GLA_KW, GLA_VW, GLA_RANK, GLA_VW)
IN_WIDTH = sum(IN_SIZES)
FFN_DIM = 2816
FFN_CONV = 3

kernel_name = "hymba_style_diffattn_rglru_gla_streaming_step"


def rms_norm(x, g):
    xf = x.astype(jnp.float32)
    y = xf * lax.rsqrt(jnp.mean(xf * xf, axis=-1, keepdims=True) + EPS)
    return (y * g.astype(jnp.float32)).astype(x.dtype)


def causal_dwconv(u, state, w, b):
    width = w.shape[0]
    up = jnp.concatenate([state.astype(u.dtype), u], axis=1)
    y = lax.conv_general_dilated(up, w[:, None, :].astype(u.dtype), window_strides=(1,), padding='VALID',
                                 dimension_numbers=('NWC', 'WIO', 'NWC'), feature_group_count=u.shape[-1])
    return y + b.astype(u.dtype), up[:, up.shape[1] - (width - 1):]


def chunk_id(pos):
    return jnp.floor_divide(pos - N_META, CHUNK)


def diff_softmax_mix(q, k, v, lam, mask):
    s = jnp.einsum('bqhcd,bkhcd->bhcqk', q, k).astype(jnp.float32) * (ATT_DH ** -0.5)
    if mask is not None:
        s = jnp.where(mask, s, -jnp.inf)
    p = jax.nn.softmax(s, axis=-1)
    pd = p[:, :, 0] - lam * p[:, :, 1]
    return jnp.einsum('bhqk,bkhe->bqhe', pd.astype(v.dtype), v)


def diff_attention_prompt(q, k, v, lam):
    b, L = q.shape[:2]
    n_blk = -(-L // Q_BLOCK)
    pad = n_blk * Q_BLOCK - L
    qp = jnp.pad(q, ((0, 0), (0, pad), (0, 0), (0, 0), (0, 0)))
    qb = qp.reshape(b, n_blk, Q_BLOCK, ATT_HEADS, 2, ATT_DH).transpose(1, 0, 2, 3, 4, 5)
    k_chunk = chunk_id(jnp.arange(L))
    starts = jnp.arange(n_blk) * Q_BLOCK

    def one_block(args):
        qblk, start = args
        q_chunk = chunk_id(start + jnp.arange(Q_BLOCK))
        mask = k_chunk[None, :] <= q_chunk[:, None]
        return diff_softmax_mix(qblk, k, v, lam, mask)

    o = lax.map(one_block, (qb, starts))
    return o.transpose(1, 0, 2, 3, 4).reshape(b, n_blk * Q_BLOCK, ATT_HEADS, ATT_VD)[:, :L]


def rg_lru(x, h0, gate_a_w, gate_a_b, gate_x_w, gate_x_b, log_lambda):
    b, L, _ = x.shape
    xb = x.reshape(b, L, LRU_BLOCKS, LRU_BW)
    r = jax.nn.sigmoid((jnp.einsum('blnc,ncd->blnd', xb, gate_a_w).reshape(b, L, LRU_WIDTH)
                        + gate_a_b).astype(jnp.float32))
    i = jax.nn.sigmoid((jnp.einsum('blnc,ncd->blnd', xb, gate_x_w).reshape(b, L, LRU_WIDTH)
                        + gate_x_b).astype(jnp.float32))
    log_a = -LRU_C * r * jax.nn.softplus(-log_lambda.astype(jnp.float32))
    a = jnp.exp(log_a)
    u = jnp.sqrt(-jnp.expm1(2.0 * log_a)) * i * x.astype(jnp.float32)

    def combine(e1, e2):
        a1, b1 = e1
        a2, b2 = e2
        return a1 * a2, a2 * b1 + b2

    A, H = lax.associative_scan(combine, (a, u), axis=1)
    h = H + A * h0.astype(jnp.float32)[:, None, :]
    return h.astype(x.dtype), h[:, -1].astype(x.dtype)


def gla_blocked(q, k, v, log_g, S0):
    b, L = q.shape[:2]
    n_blk = -(-L // GLA_BLOCK)
    pad = n_blk * GLA_BLOCK - L

    def blocks(t):
        t = jnp.pad(t.astype(jnp.float32), ((0, 0), (0, pad), (0, 0), (0, 0)))
        return t.reshape(b, n_blk, GLA_BLOCK, GLA_HEADS, t.shape[-1]).transpose(1, 0, 3, 2, 4)

    qs = blocks(q * (GLA_DK ** -0.5))
    ks, vs, gs = blocks(k), blocks(v), blocks(log_g)
    causal = jnp.tril(jnp.ones((GLA_BLOCK, GLA_BLOCK), bool))

    def step(S, blk):
        qc, kc, vc, gc = blk
        bcum = jnp.cumsum(gc, axis=2)
        o_inter = jnp.einsum('bhcd,bhde->bhce', qc * jnp.exp(bcum), S)
        rel = bcum[:, :, :, None, :] - bcum[:, :, None, :, :]
        decay = jnp.exp(jnp.where(causal[:, :, None], rel, -jnp.inf))
        att = jnp.einsum('bhid,bhjd,bhijd->bhij', qc, kc, decay)
        o = o_inter + jnp.einsum('bhij,bhje->bhie', att, vc)
        b_last = bcum[:, :, -1:, :]
        S_new = (jnp.exp(b_last[:, :, 0, :])[..., None] * S
                 + jnp.einsum('bhcd,bhce->bhde', kc * jnp.exp(b_last - bcum), vc))
        return S_new, o

    S, o = lax.scan(step, S0.astype(jnp.float32), (qs, ks, vs, gs))
    o = o.transpose(1, 0, 3, 2, 4).reshape(b, n_blk * GLA_BLOCK, GLA_HEADS, GLA_DV)[:, :L]
    return o, S


def run_trunk(x, prm, states, is_prompt):
    b, L, _ = x.shape
    dt = x.dtype
    split_points = np.cumsum(IN_SIZES)[:-1].tolist()
    new_k, new_v, new_lh, new_lc, new_S, new_fc = [], [], [], [], [], []
    for l in range(DEPTH):
        if is_prompt:
            lru_h0 = jnp.zeros((b, LRU_WIDTH), dt)
            lru_c0 = jnp.zeros((b, LRU_CONV - 1, LRU_WIDTH), dt)
            S0 = jnp.zeros((b, GLA_HEADS, GLA_DK, GLA_DV), dt)
            ffn_c0 = jnp.zeros((b, FFN_CONV - 1, FFN_DIM), dt)
        else:
            ck, cv = states[0][l], states[1][l]
            lru_h0, lru_c0, S0, ffn_c0 = states[2][l], states[3][l], states[4][l], states[5][l]

        hn = rms_norm(x, prm['norm_mix_g'][l])
        proj = hn @ prm['w_in'][l]
        aq, ak, av, lx, lg, gq, gk, gv, glr, gog = jnp.split(proj, split_points, axis=-1)

        q = aq.reshape(b, L, ATT_HEADS, 2, ATT_DH)
        k = ak.reshape(b, L, ATT_HEADS, 2 * ATT_DH)
        v = av.reshape(b, L, ATT_HEADS, ATT_VD)
        lam_init = 0.8 - 0.6 * math.exp(-0.3 * l)
        lqk = prm['attn_lambda'][l].astype(jnp.float32)
        lam = jnp.exp(jnp.sum(lqk[0] * lqk[1])) - jnp.exp(jnp.sum(lqk[2] * lqk[3])) + lam_init
        if is_prompt:
            o_att = diff_attention_prompt(q, k.reshape(b, L, ATT_HEADS, 2, ATT_DH), v, lam)
        else:
            k_all = jnp.concatenate([ck.astype(dt), k], axis=1)
            v_all = jnp.concatenate([cv.astype(dt), v], axis=1)
            o_att = diff_softmax_mix(q, k_all.reshape(b, k_all.shape[1], ATT_HEADS, 2, ATT_DH), v_all, lam, None)
        o_att = (rms_norm(o_att, prm['attn_subln_g'][l]) * (1.0 - lam_init)).reshape(b, L, ATT_WIDTH)

        xc, lru_c1 = causal_dwconv(lx, lru_c0, prm['lru_conv_w'][l], prm['lru_conv_b'][l])
        hseq, lru_h1 = rg_lru(xc, lru_h0, prm['lru_gate_a_w'][l], prm['lru_gate_a_b'][l],
                              prm['lru_gate_x_w'][l], prm['lru_gate_x_b'][l], prm['lru_log_lambda'][l])
        o_lru = hseq * jax.nn.gelu(lg)

        log_g = jax.nn.log_sigmoid((glr @ prm['gla_gate_w2'][l] + prm['gla_gate_b'][l]).astype(jnp.float32)) / GLA_TAU
        o_gla, S1 = gla_blocked(gq.reshape(b, L, GLA_HEADS, GLA_DK), gk.reshape(b, L, GLA_HEADS, GLA_DK),
                                gv.reshape(b, L, GLA_HEADS, GLA_DV), log_g.reshape(b, L, GLA_HEADS, GLA_DK), S0)
        o_gla = rms_norm(o_gla.astype(dt), prm['gla_norm_g'][l]).reshape(b, L, GLA_VW) * jax.nn.silu(gog)

        x = x + jnp.concatenate([o_att, o_lru, o_gla], axis=-1) @ prm['w_out'][l]

        hn = rms_norm(x, prm['norm_ffn_g'][l])
        u = hn @ prm['ffn_w_up'][l]
        uc, ffn_c1 = causal_dwconv(u, ffn_c0, prm['ffn_conv_w'][l], prm['ffn_conv_b'][l])
        x = x + (jax.nn.gelu(uc) * (hn @ prm['ffn_w_gate'][l])) @ prm['ffn_w_down'][l]

        new_k.append(k)
        new_v.append(v)
        new_lh.append(lru_h1)
        new_lc.append(lru_c1)
        new_S.append(S1.astype(dt))
        new_fc.append(ffn_c1)
    y = rms_norm(x, prm['norm_final_g'])
    return y, (jnp.stack(new_k), jnp.stack(new_v), jnp.stack(new_lh), jnp.stack(new_lc),
               jnp.stack(new_S), jnp.stack(new_fc))


def setup_inputs(seed: int = 0) -> dict:
    key = jax.random.key(seed)
    ks = jax.random.split(key, 40)
    f32 = jnp.float32

    def nrm(i, shape, scale):
        return jax.random.normal(ks[i], shape, f32) * scale

    a0 = jax.random.uniform(ks[39], (DEPTH, LRU_WIDTH), f32, minval=0.9, maxval=0.999) ** (1.0 / LRU_C)
    return {
        "x_prompt": nrm(0, (BATCH, SEQ, D_MODEL), 1.0),
        "x_sample": nrm(1, (DEC_BATCH, DEC_SEQ, D_MODEL), 1.0),
        "cache_attn_k": nrm(2, (DEPTH, DEC_BATCH, PAST_LEN, ATT_HEADS, 2 * ATT_DH), 1.0),
        "cache_attn_v": nrm(3, (DEPTH, DEC_BATCH, PAST_LEN, ATT_HEADS, ATT_VD), 1.0),
        "state_lru_h": nrm(4, (DEPTH, DEC_BATCH, LRU_WIDTH), 0.5),
        "state_lru_conv": nrm(5, (DEPTH, DEC_BATCH, LRU_CONV - 1, LRU_WIDTH), 1.0),
        "state_gla": nrm(6, (DEPTH, DEC_BATCH, GLA_HEADS, GLA_DK, GLA_DV), 0.5),
        "state_ffn_conv": nrm(7, (DEPTH, DEC_BATCH, FFN_CONV - 1, FFN_DIM), 1.0),
        "meta_tokens": nrm(8, (N_META, D_MODEL), 1.0),
        "norm_mix_g": 1.0 + nrm(9, (DEPTH, D_MODEL), 0.02),
        "w_in": nrm(10, (DEPTH, D_MODEL, IN_WIDTH), D_MODEL ** -0.5),
        "attn_lambda": nrm(11, (DEPTH, 4, ATT_DH), 0.1),
        "attn_subln_g": 1.0 + nrm(12, (DEPTH, ATT_VD), 0.02),
        "lru_conv_w": nrm(13, (DEPTH, LRU_CONV, LRU_WIDTH), LRU_CONV ** -0.5),
        "lru_conv_b": nrm(14, (DEPTH, LRU_WIDTH), 0.02),
        "lru_gate_a_w": nrm(15, (DEPTH, LRU_BLOCKS, LRU_BW, LRU_BW), LRU_BW ** -0.5),
        "lru_gate_a_b": nrm(16, (DEPTH, LRU_WIDTH), 0.02),
        "lru_gate_x_w": nrm(17, (DEPTH, LRU_BLOCKS, LRU_BW, LRU_BW), LRU_BW ** -0.5),
        "lru_gate_x_b": nrm(18, (DEPTH, LRU_WIDTH), 0.02),
        "lru_log_lambda": jnp.log(a0) - jnp.log1p(-a0),
        "gla_gate_w2": nrm(19, (DEPTH, GLA_RANK, GLA_KW), GLA_RANK ** -0.5),
        "gla_gate_b": nrm(20, (DEPTH, GLA_KW), 0.1),
        "gla_norm_g": 1.0 + nrm(21, (DEPTH, GLA_DV), 0.02),
        "w_out": nrm(22, (DEPTH, MIX_WIDTH, D_MODEL), MIX_WIDTH ** -0.5),
        "norm_ffn_g": 1.0 + nrm(23, (DEPTH, D_MODEL), 0.02),
        "ffn_w_up": nrm(24, (DEPTH, D_MODEL, FFN_DIM), D_MODEL ** -0.5),
        "ffn_conv_w": nrm(25, (DEPTH, FFN_CONV, FFN_DIM), FFN_CONV ** -0.5),
        "ffn_conv_b": nrm(26, (DEPTH, FFN_DIM), 0.02),
        "ffn_w_gate": nrm(27, (DEPTH, D_MODEL, FFN_DIM), D_MODEL ** -0.5),
        "ffn_w_down": nrm(28, (DEPTH, FFN_DIM, D_MODEL), FFN_DIM ** -0.5),
        "norm_final_g": 1.0 + nrm(29, (D_MODEL,), 0.02),
    }


def reference(x_prompt, x_sample, cache_attn_k, cache_attn_v, state_lru_h, state_lru_conv, state_gla,
              state_ffn_conv, meta_tokens, norm_mix_g, w_in, attn_lambda, attn_subln_g, lru_conv_w, lru_conv_b,
              lru_gate_a_w, lru_gate_a_b, lru_gate_x_w, lru_gate_x_b, lru_log_lambda, gla_gate_w2, gla_gate_b,
              gla_norm_g, w_out, norm_ffn_g, ffn_w_up, ffn_conv_w, ffn_conv_b, ffn_w_gate, ffn_w_down,
              norm_final_g):
    prm = {
        'norm_mix_g': norm_mix_g, 'w_in': w_in, 'attn_lambda': attn_lambda, 'attn_subln_g': attn_subln_g,
        'lru_conv_w': lru_conv_w, 'lru_conv_b': lru_conv_b, 'lru_gate_a_w': lru_gate_a_w,
        'lru_gate_a_b': lru_gate_a_b, 'lru_gate_x_w': lru_gate_x_w, 'lru_gate_x_b': lru_gate_x_b,
        'lru_log_lambda': lru_log_lambda, 'gla_gate_w2': gla_gate_w2, 'gla_gate_b': gla_gate_b,
        'gla_norm_g': gla_norm_g, 'w_out': w_out, 'norm_ffn_g': norm_ffn_g, 'ffn_w_up': ffn_w_up,
        'ffn_conv_w': ffn_conv_w, 'ffn_conv_b': ffn_conv_b, 'ffn_w_gate': ffn_w_gate,
        'ffn_w_down': ffn_w_down, 'norm_final_g': norm_final_g,
    }
    b = x_prompt.shape[0]
    meta = jnp.broadcast_to(meta_tokens.astype(x_prompt.dtype)[None], (b, N_META, D_MODEL))
    xp = jnp.concatenate([meta, x_prompt], axis=1)
    yp, (k_p, v_p, lh_p, lc_p, S_p, fc_p) = run_trunk(xp, prm, None, True)
    y_prompt = yp[:, N_META:]
    y_sample, (k_s, v_s, lh_s, lc_s, S_s, fc_s) = run_trunk(
        x_sample, prm, (cache_attn_k, cache_attn_v, state_lru_h, state_lru_conv, state_gla, state_ffn_conv), False)
    return (y_prompt, y_sample, k_p, v_p, lh_p, lc_p, S_p, fc_p, k_s, v_s, lh_s, lc_s, S_s, fc_s)
```

```python
import functools
import math

import jax
import jax.numpy as jnp
from jax import lax
from jax.experimental import pallas as pl
from jax.experimental.pallas import tpu as pltpu

F32 = jnp.float32
BF16 = jnp.bfloat16

CHUNK = 64
EPS = 1e-6
ATT_DH = 64
ATT_VD = 128
LRU_C = 8.0
GLA_DK = 32
GLA_DV = 64
GLA_RANK = 16
GLA_TAU = 16.0

PROJ_TN = 512
PROJ_W = 3072
QKV_W = 1536
COL_LRU = 3
COL_GLA_A = 4
COL_GLA_B = 5

VMEM_LIMIT = 48 * 1024 * 1024


def _rms(x, g):
    return x * lax.rsqrt(jnp.mean(x * x, axis=-1, keepdims=True) + EPS) * g


def _params(*sem):
    return pltpu.CompilerParams(dimension_semantics=sem, vmem_limit_bytes=VMEM_LIMIT)


def _in_proj_kernel(x_ref, g_ref, w_ref, of_ref, ob_ref, hn_ref, *, n_bf_tiles, q_scale):
    j = pl.program_id(1)

    @pl.when(j == 0)
    def _():
        hn_ref[...] = _rms(x_ref[...], g_ref[...]).astype(BF16)

    y = jnp.dot(hn_ref[...], w_ref[...], preferred_element_type=F32)
    of_ref[...] = y

    @pl.when(j < n_bf_tiles)
    def _():
        scale = jnp.where(j == 0, q_scale, 1.0).astype(F32)
        ob_ref[...] = (y * scale).astype(BF16)


def _in_proj(x2, g, w, tm):
    m, d = x2.shape
    n_bf = QKV_W // PROJ_TN
    kern = functools.partial(_in_proj_kernel, n_bf_tiles=n_bf, q_scale=ATT_DH ** -0.5)
    return pl.pallas_call(
        kern,
        grid=(m // tm, PROJ_W // PROJ_TN),
        in_specs=[
            pl.BlockSpec((tm, d), lambda i, j: (i, 0)),
            pl.BlockSpec((1, d), lambda i, j: (0, 0)),
            pl.BlockSpec((d, PROJ_TN), lambda i, j: (0, j)),
        ],
        out_specs=[
            pl.BlockSpec((tm, PROJ_TN), lambda i, j: (i, j)),
            pl.BlockSpec((tm, PROJ_TN), lambda i, j: (i, jnp.minimum(j, n_bf - 1))),
        ],
        out_shape=[jax.ShapeDtypeStruct((m, PROJ_W), F32), jax.ShapeDtypeStruct((m, QKV_W), BF16)],
        scratch_shapes=[pltpu.VMEM((tm, d), BF16)],
        compiler_params=_params("parallel", "arbitrary"),
        name="in_proj",
    )(x2, g.reshape(1, d), w)


def _attn_kernel(*refs, tq, tk, n_pre, n_pre_valid, tkp, lam_init, has_pre):
    if has_pre:
        lam_ref, g_ref, q_ref, k_ref, v_ref, kp_ref, vp_ref, o_ref, m_ref, l_ref, acc_ref = refs
    else:
        lam_ref, g_ref, q_ref, k_ref, v_ref, o_ref, m_ref, l_ref, acc_ref = refs
    qi = pl.program_id(2)

    q = q_ref[0]
    lane = lax.broadcasted_iota(jnp.int32, q.shape, 1)
    zero = jnp.zeros_like(q)
    q2 = jnp.concatenate([jnp.where(lane < ATT_DH, q, zero), jnp.where(lane >= ATT_DH, q, zero)], axis=0)

    m_ref[...] = jnp.full(m_ref.shape, -jnp.inf, F32)
    l_ref[...] = jnp.zeros(l_ref.shape, F32)
    acc_ref[...] = jnp.zeros(acc_ref.shape, F32)

    def update(kt, vt, mask):
        s = lax.dot_general(q2, kt.astype(BF16), (((1,), (1,)), ((), ())), preferred_element_type=F32)
        if mask is not None:
            s = jnp.where(mask, s, -jnp.inf)
        m_old = m_ref[...]
        m_new = jnp.maximum(m_old, jnp.max(s, axis=-1, keepdims=True))
        alpha = jnp.exp(m_old - m_new)
        p = jnp.exp(s - m_new)
        l_ref[...] = alpha * l_ref[...] + jnp.sum(p, axis=-1, keepdims=True)
        m_ref[...] = m_new
        acc_ref[...] = alpha * acc_ref[...] + jnp.dot(p.astype(BF16), vt.astype(BF16),
                                                      preferred_element_type=F32)

    if has_pre:
        n_tiles = n_pre // tkp
        full_tiles = n_pre_valid // tkp
        if full_tiles > 0:
            def pre_body(t, c):
                off = pl.multiple_of(t * tkp, tkp)
                update(kp_ref[0, pl.ds(off, tkp), :], vp_ref[0, pl.ds(off, tkp), :], None)
                return c
            lax.fori_loop(0, full_tiles, pre_body, 0)
        if full_tiles < n_tiles:
            col = lax.broadcasted_iota(jnp.int32, (2 * tq, tkp), 1)
            off = full_tiles * tkp
            update(kp_ref[0, off:off + tkp, :], vp_ref[0, off:off + tkp, :],
                   col < (n_pre_valid - off))

    def own_body(t, c):
        off = pl.multiple_of(t * tk, tk)
        update(k_ref[0, pl.ds(off, tk), :], v_ref[0, pl.ds(off, tk), :], None)
        return c
    lax.fori_loop(0, qi * (tq // tk), own_body, 0)

    off = pl.multiple_of(qi * tq, tq)
    if tq > CHUNK:
        row = lax.broadcasted_iota(jnp.int32, (2 * tq, tq), 0)
        col = lax.broadcasted_iota(jnp.int32, (2 * tq, tq), 1)
        row = jnp.where(row >= tq, row - tq, row)
        mask = (col // CHUNK) <= (row // CHUNK)
    else:
        mask = None
    update(k_ref[0, pl.ds(off, tq), :], v_ref[0, pl.ds(off, tq), :], mask)

    lq = lam_ref[...]
    lam = (jnp.exp(jnp.sum(lq[0:1] * lq[1:2], axis=-1, keepdims=True))
           - jnp.exp(jnp.sum(lq[2:3] * lq[3:4], axis=-1, keepdims=True)) + lam_init)
    acc = acc_ref[...]
    l = l_ref[...]
    o = acc[:tq] / l[:tq] - lam * (acc[tq:] / l[tq:])
    o_ref[0] = _rms(o, g_ref[...]) * (1.0 - lam_init)


def _attention(qkv, pre_k, pre_v, n_pre_valid, lam_p, g, lam_init, tq, heads):
    b, L, _ = qkv.shape
    has_pre = pre_k is not None
    tk = tq
    kern_kw = dict(tq=tq, tk=tk, lam_init=lam_init, has_pre=has_pre, n_pre=0, n_pre_valid=0, tkp=0)
    in_specs = [
        pl.BlockSpec((4, ATT_DH), lambda bi, h, qi: (0, 0)),
        pl.BlockSpec((1, ATT_VD), lambda bi, h, qi: (0, 0)),
        pl.BlockSpec((1, tq, 128), lambda bi, h, qi: (bi, qi, h)),
        pl.BlockSpec((1, L, 128), lambda bi, h, qi: (bi, 0, heads + h)),
        pl.BlockSpec((1, L, 128), lambda bi, h, qi: (bi, 0, 2 * heads + h)),
    ]
    args = [lam_p, g.reshape(1, ATT_VD), qkv, qkv, qkv]
    if has_pre:
        lp = pre_k.shape[1]
        tkp = min(lp, 512)
        kern_kw.update(n_pre=lp, n_pre_valid=n_pre_valid, tkp=tkp)
        in_specs += [pl.BlockSpec((1, lp, 128), lambda bi, h, qi: (bi, 0, h)),
                     pl.BlockSpec((1, lp, 128), lambda bi, h, qi: (bi, 0, h))]
        args += [pre_k, pre_v]
    return pl.pallas_call(
        functools.partial(_attn_kernel, **kern_kw),
        grid=(b, heads, L // tq),
        in_specs=in_specs,
        out_specs=pl.BlockSpec((1, tq, 128), lambda bi, h, qi: (bi, qi, h)),
        out_shape=jax.ShapeDtypeStruct((b, L, heads * ATT_VD), F32),
        scratch_shapes=[pltpu.VMEM((2 * tq, 1), F32), pltpu.VMEM((2 * tq, 1), F32),
                        pltpu.VMEM((2 * tq, ATT_VD), F32)],
        compiler_params=_params("parallel", "parallel", "arbitrary"),
        name="diff_attn",
    )(*args)


def _lru_kernel(p_ref, c0_ref, h0_ref, cw_ref, cb_ref, wa_ref, ba_ref, wx_ref, bx_ref, ll_ref,
                o_ref, hout_ref, cout_ref, ext_ref, hc_ref, *, tl, width, conv):
    li = pl.program_id(1)
    pad = 8

    @pl.when(li == 0)
    def _():
        ext_ref[pad - (conv - 1):pad] = c0_ref[0]
        hc_ref[...] = h0_ref[0]

    lx = p_ref[0, :, 0:width]
    lg = p_ref[0, :, width:2 * width]
    ext_ref[pad:pad + tl] = lx
    cw = cw_ref[...]
    xc = cb_ref[...] + cw[conv - 1:conv] * lx
    for s in range(1, conv):
        xc = xc + cw[conv - 1 - s:conv - s] * ext_ref[pad - s:pad - s + tl]
    tail = ext_ref[pad + tl - (conv - 1):pad + tl]
    cout_ref[0] = tail
    ext_ref[pad - (conv - 1):pad] = tail

    xb = xc.astype(BF16)
    r = jax.nn.sigmoid(jnp.dot(xb, wa_ref[...], preferred_element_type=F32) + ba_ref[...])
    i = jax.nn.sigmoid(jnp.dot(xb, wx_ref[...], preferred_element_type=F32) + bx_ref[...])
    z = -ll_ref[...]
    softplus = jnp.maximum(z, 0.0) + jnp.log1p(jnp.exp(-jnp.abs(z)))
    log_a = -LRU_C * r * softplus
    a = jnp.exp(log_a)
    u = jnp.sqrt(-jnp.tanh(log_a) * (1.0 + a * a)) * i * xc

    row = lax.broadcasted_iota(jnp.int32, a.shape, 0)
    d = 1
    while d < tl:
        keep = row >= d
        a_prev = jnp.where(keep, pltpu.roll(a, d, 0), 1.0)
        u_prev = jnp.where(keep, pltpu.roll(u, d, 0), 0.0)
        u = u + a * u_prev
        a = a * a_prev
        d *= 2
    h = u + a * hc_ref[...]
    hc_ref[...] = h[tl - 1:tl]
    hout_ref[0] = h[tl - 1:tl]
    o_ref[0] = h * jax.nn.gelu(lg)


def _lru(proj, c0, h0, cw, cb, wa, ba, wx, bx, ll, tl):
    b, L, _ = proj.shape
    conv, width = cw.shape
    vec = lambda a: a.reshape(1, width)
    const = lambda shape: pl.BlockSpec(shape, lambda bi, li: (0,) * len(shape))
    kern = functools.partial(_lru_kernel, tl=tl, width=width, conv=conv)
    return pl.pallas_call(
        kern,
        grid=(b, L // tl),
        in_specs=[
            pl.BlockSpec((1, tl, 2 * width), lambda bi, li: (bi, li, COL_LRU)),
            pl.BlockSpec((1, conv - 1, width), lambda bi, li: (bi, 0, 0)),
            pl.BlockSpec((1, 1, width), lambda bi, li: (bi, 0, 0)),
            const((conv, width)), const((1, width)),
            const((width, width)), const((1, width)),
            const((width, width)), const((1, width)), const((1, width)),
        ],
        out_specs=[
            pl.BlockSpec((1, tl, width), lambda bi, li: (bi, li, 0)),
            pl.BlockSpec((1, 1, width), lambda bi, li: (bi, 0, 0)),
            pl.BlockSpec((1, conv - 1, width), lambda bi, li: (bi, 0, 0)),
        ],
        out_shape=[jax.ShapeDtypeStruct((b, L, width), F32),
                   jax.ShapeDtypeStruct((b, 1, width), F32),
                   jax.ShapeDtypeStruct((b, conv - 1, width), F32)],
        scratch_shapes=[pltpu.VMEM((tl + 8, width), F32), pltpu.VMEM((1, width), F32)],
        compiler_params=_params("parallel", "arbitrary"),
        name="rg_lru",
    )(proj, c0, h0.reshape(b, 1, width), cw, vec(cb), wa, vec(ba), wx, vec(bx), vec(ll))


def _gla_kernel(pa_ref, pb_ref, s0_ref, w2_ref, gb_ref, gn_ref, o_ref, sout_ref, st_ref, oi_ref,
                *, tl, chunk, kw, vw):
    li = pl.program_id(1)

    @pl.when(li == 0)
    def _():
        st_ref[...] = s0_ref[0]

    q = pa_ref[0, :, 0:kw] * (GLA_DK ** -0.5)
    k = pa_ref[0, :, kw:2 * kw]
    v = pa_ref[0, :, 2 * kw:2 * kw + vw]
    gog = pb_ref[0, :, 0:vw]
    glr = pb_ref[0, :, vw:2 * vw]

    x = jnp.dot(glr.astype(BF16), w2_ref[...], preferred_element_type=F32) + gb_ref[...]
    log_g = (jnp.minimum(x, 0.0) - jnp.log1p(jnp.exp(-jnp.abs(x)))) / GLA_TAU

    rowc = lax.broadcasted_iota(jnp.int32, (tl, kw), 0) % chunk
    bc = log_g
    d = 1
    while d < chunk:
        bc = bc + jnp.where(rowc >= d, pltpu.roll(bc, d, 0), 0.0)
        d *= 2

    hk = lax.broadcasted_iota(jnp.int32, (kw, vw), 0) // GLA_DK
    hv = lax.broadcasted_iota(jnp.int32, (kw, vw), 1) // GLA_DV
    spread = jnp.where(hk == hv, 1.0, 0.0).astype(BF16)

    oi_ref[...] = jnp.zeros(oi_ref.shape, F32)

    def band(dl, c):
        valid = rowc >= dl
        rel = jnp.where(valid, bc - pltpu.roll(bc, dl, 0), -jnp.inf)
        w = q * pltpu.roll(k, dl, 0) * jnp.exp(rel)
        att = jnp.dot(w.astype(BF16), spread, preferred_element_type=F32)
        oi_ref[...] += att * pltpu.roll(v, dl, 0)
        return c
    lax.fori_loop(0, chunk, band, 0)

    hs = lax.broadcasted_iota(jnp.int32, (vw, kw), 0) // GLA_DV
    hl = lax.broadcasted_iota(jnp.int32, (vw, kw), 1) // GLA_DK
    same_head = hs == hl
    for c in range(tl // chunk):
        rows = slice(c * chunk, (c + 1) * chunk)
        b_c = bc[rows]
        b_last = b_c[chunk - 1:chunk]
        st = st_ref[...]
        qe = (q[rows] * jnp.exp(b_c)).astype(BF16)
        o_inter = lax.dot_general(qe, st.astype(BF16), (((1,), (1,)), ((), ())),
                                  preferred_element_type=F32)
        ke = (k[rows] * jnp.exp(b_last - b_c)).astype(BF16)
        kv_t = lax.dot_general(v[rows].astype(BF16), ke, (((0,), (0,)), ((), ())),
                               preferred_element_type=F32)
        st_ref[...] = jnp.exp(b_last) * st + jnp.where(same_head, kv_t, 0.0)
        oi_ref[rows, :] += o_inter
    sout_ref[0] = st_ref[...]

    o = oi_ref[...]
    o2 = o * o
    lane_h = lax.broadcasted_iota(jnp.int32, o.shape, 1) // GLA_DV
    ms = jnp.zeros_like(o)
    for h in range(vw // GLA_DV):
        in_h = lane_h == h
        s_h = jnp.sum(jnp.where(in_h, o2, 0.0), axis=-1, keepdims=True) * (1.0 / GLA_DV)
        ms = jnp.where(in_h, s_h, ms)
    o_ref[0] = o * lax.rsqrt(ms + EPS) * gn_ref[...] * jax.nn.silu(gog)


def _gla(proj, s0_t, w2p, gb, gn_t, tl):
    b, L, _ = proj.shape
    vw, kw = s0_t.shape[1:]
    chunk = min(CHUNK, L)
    const = lambda shape: pl.BlockSpec(shape, lambda bi, li: (0,) * len(shape))
    kern = functools.partial(_gla_kernel, tl=tl, chunk=chunk, kw=kw, vw=vw)
    return pl.pallas_call(
        kern,
        grid=(b, L // tl),
        in_specs=[
            pl.BlockSpec((1, tl, PROJ_TN), lambda bi, li: (bi, li, COL_GLA_A)),
            pl.BlockSpec((1, tl, PROJ_TN), lambda bi, li: (bi, li, COL_GLA_B)),
            pl.BlockSpec((1, vw, kw), lambda bi, li: (bi, 0, 0)),
            const((vw, kw)), const((1, kw)), const((1, vw)),
        ],
        out_specs=[
            pl.BlockSpec((1, tl, vw), lambda bi, li: (bi, li, 0)),
            pl.BlockSpec((1, vw, kw), lambda bi, li: (bi, 0, 0)),
        ],
        out_shape=[jax.ShapeDtypeStruct((b, L, vw), F32), jax.ShapeDtypeStruct((b, vw, kw), F32)],
        scratch_shapes=[pltpu.VMEM((vw, kw), F32), pltpu.VMEM((tl, vw), F32)],
        compiler_params=_params("parallel", "arbitrary"),
        name="gla",
    )(proj, proj, s0_t, w2p, gb.reshape(1, kw), gn_t.reshape(1, vw))


def _ffn_kernel(x_ref, oa_ref, ol_ref, og_ref, wo_ref, g_ref, c0_ref, wu_ref, wg_ref, cw_ref, cb_ref,
                wd_ref, gf_ref, out_ref, cout_ref, acc_ref, hn_ref, carry_ref,
                *, bb, tl, tn, conv, final_norm):
    li = pl.program_id(1)
    j = pl.program_id(2)
    m = bb * tl
    d = x_ref.shape[-1]
    wa = oa_ref.shape[-1]
    wl = ol_ref.shape[-1]

    @pl.when(j == 0)
    def _():
        mix = jnp.dot(oa_ref[...].reshape(m, wa).astype(BF16), wo_ref[0:wa], preferred_element_type=F32)
        mix += jnp.dot(ol_ref[...].reshape(m, wl).astype(BF16), wo_ref[wa:wa + wl],
                       preferred_element_type=F32)
        mix += jnp.dot(og_ref[...].reshape(m, d - wa - wl).astype(BF16), wo_ref[wa + wl:d],
                       preferred_element_type=F32)
        xn = x_ref[...].reshape(m, d) + mix
        acc_ref[...] = xn
        hn_ref[...] = _rms(xn, g_ref[...]).astype(BF16)

    @pl.when(li == 0)
    def _():
        carry_ref[j] = c0_ref[...]

    hn = hn_ref[...]
    u = jnp.dot(hn, wu_ref[...], preferred_element_type=F32)
    gate = jnp.dot(hn, wg_ref[...], preferred_element_type=F32)

    prev = carry_ref[j]
    t = lax.broadcasted_iota(jnp.int32, (bb, tl, tn), 1)
    cw = cw_ref[...]
    u3 = u.reshape(bb, tl, tn)
    uc = cb_ref[...] + cw[conv - 1:conv] * u3
    for s in range(1, conv):
        shifted = pltpu.roll(u, s, 0).reshape(bb, tl, tn)
        for r in range(s):
            shifted = jnp.where(t == r, prev[:, conv - 1 - s + r:conv - s + r, :], shifted)
        uc = uc + cw[conv - 1 - s:conv - s] * shifted
    tail = u3[:, tl - (conv - 1):tl, :]
    carry_ref[j] = tail
    cout_ref[:, j] = tail

    act = (jax.nn.gelu(uc) * gate.reshape(bb, tl, tn)).reshape(m, tn).astype(BF16)
    acc_ref[...] += jnp.dot(act, wd_ref[...], preferred_element_type=F32)

    @pl.when(j == pl.num_programs(2) - 1)
    def _():
        y = acc_ref[...]
        if final_norm:
            y = _rms(y, gf_ref[...])
        out_ref[...] = y.reshape(bb, tl, d)


def _ffn(x, oa, ol, og, wo, g, c0, wu, wg, cw, cb, wd, gf, bb, tl, tn, final_norm):
    b, L, d = x.shape
    conv, f = cw.shape
    nj = f // tn
    kern = functools.partial(_ffn_kernel, bb=bb, tl=tl, tn=tn, conv=conv, final_norm=final_norm)
    act_spec = lambda w: pl.BlockSpec((bb, tl, w), lambda bi, li, j: (bi, li, 0))
    const = lambda shape: pl.BlockSpec(shape, lambda bi, li, j: (0,) * len(shape))
    return pl.pallas_call(
        kern,
        grid=(b // bb, L // tl, nj),
        in_specs=[
            act_spec(d), act_spec(oa.shape[-1]), act_spec(ol.shape[-1]), act_spec(og.shape[-1]),
            const((d, d)), const((1, d)),
            pl.BlockSpec((bb, conv - 1, tn), lambda bi, li, j: (bi, 0, j)),
            pl.BlockSpec((d, tn), lambda bi, li, j: (0, j)),
            pl.BlockSpec((d, tn), lambda bi, li, j: (0, j)),
            pl.BlockSpec((conv, tn), lambda bi, li, j: (0, j)),
            pl.BlockSpec((1, tn), lambda bi, li, j: (0, j)),
            pl.BlockSpec((tn, d), lambda bi, li, j: (j, 0)),
            const((1, d)),
        ],
        out_specs=[
            pl.BlockSpec((bb, tl, d), lambda bi, li, j: (bi, li, 0)),
            pl.BlockSpec((bb, nj, conv - 1, tn), lambda bi, li, j: (bi, 0, 0, 0)),
        ],
        out_shape=[jax.ShapeDtypeStruct((b, L, d), F32), jax.ShapeDtypeStruct((b, nj, conv - 1, tn), F32)],
        scratch_shapes=[pltpu.VMEM((bb * tl, d), F32), pltpu.VMEM((bb * tl, d), BF16),
                        pltpu.VMEM((nj, bb, conv - 1, tn), F32)],
        compiler_params=_params("parallel", "arbitrary", "arbitrary"),
        name="mix_ffn",
    )(x, oa, ol, og, wo, g.reshape(1, d), c0, wu, wg, cw, cb.reshape(1, f), wd, gf.reshape(1, d))


def _block_diag(w):
    n, c, dd = w.shape
    eye = jnp.eye(n, dtype=w.dtype)
    return jnp.einsum('ncd,nm->ncmd', w, eye).reshape(n * c, n * dd)


def _prep_layer(l, w_in, lru_gate_a_w, lru_gate_x_w, gla_gate_w2, gla_norm_g, w_out, ffn_w_up,
                ffn_w_gate, ffn_w_down, sizes):
    att_w, lru_w, gla_kw, gla_vw = sizes
    o = 0
    parts = {}
    for name, wdt in (("q", att_w), ("k", att_w), ("v", att_w), ("lx", lru_w), ("lg", lru_w),
                      ("gq", gla_kw), ("gk", gla_kw), ("gv", gla_vw), ("glr", GLA_RANK), ("gog", gla_vw)):
        parts[name] = w_in[l][:, o:o + wdt]
        o += wdt
    d = w_in.shape[1]
    pad = jnp.zeros((d, PROJ_W - o), w_in.dtype)
    order = ("q", "k", "v", "lx", "lg", "gq", "gk", "gv", "gog", "glr")
    w_in_p = jnp.concatenate([parts[n] for n in order] + [pad], axis=1).astype(BF16)
    w2p = jnp.zeros((gla_vw, gla_kw), F32).at[:GLA_RANK].set(gla_gate_w2[l]).astype(BF16)
    return dict(
        w_in=w_in_p,
        wa=_block_diag(lru_gate_a_w[l]).astype(BF16),
        wx=_block_diag(lru_gate_x_w[l]).astype(BF16),
        w2p=w2p,
        gn_t=jnp.tile(gla_norm_g[l], gla_vw // GLA_DV),
        w_out=w_out[l].astype(BF16),
        wu=ffn_w_up[l].astype(BF16), wg=ffn_w_gate[l].astype(BF16), wd=ffn_w_down[l].astype(BF16),
    )


def _state_to_bd(s):
    b, h, dk, dv = s.shape
    eye = jnp.eye(h, dtype=s.dtype)
    return jnp.einsum('bhde,hg->bhegd', s, eye).reshape(b, h * dv, h * dk)


def _state_from_bd(st, h):
    b, vw, kw = st.shape
    dv, dk = vw // h, kw // h
    s5 = st.reshape(b, h, dv, h, dk)
    diag = jnp.stack([s5[:, i, :, i, :] for i in range(h)], axis=1)
    return diag.transpose(0, 1, 3, 2)


def _pick_tile(n, target):
    t = min(n, target)
    while n % t:
        t //= 2
    return t


def _trunk(x, states, pre_kv, n_pre_valid, layers, prm, heads, gla_heads, tiles):
    b, L, d = x.shape
    depth = len(layers)
    tm, tq, tl_lru, tl_gla, bb, tl_ffn, tn = tiles
    outs = []
    for l in range(depth):
        lw = layers[l]
        width = prm['lru_conv_w'].shape[-1]
        f = prm['ffn_conv_w'].shape[-1]
        if states is None:
            h0 = jnp.zeros((b, width), F32)
            c0 = jnp.zeros((b, prm['lru_conv_w'].shape[1] - 1, width), F32)
            s0 = jnp.zeros((b, gla_heads, GLA_DK, GLA_DV), F32)
            fc0 = jnp.zeros((b, prm['ffn_conv_w'].shape[1] - 1, f), F32)
        else:
            h0, c0, s0, fc0 = states[l]
        pf, pb = _in_proj(x.reshape(b * L, d), prm['norm_mix_g'][l], lw['w_in'], tm)
        pf = pf.reshape(b, L, PROJ_W)
        pb = pb.reshape(b, L, QKV_W)
        lam_init = 0.8 - 0.6 * math.exp(-0.3 * l)
        pk, pv = pre_kv[l] if pre_kv is not None else (None, None)
        o_att = _attention(pb, pk, pv, n_pre_valid, prm['attn_lambda'][l], prm['attn_subln_g'][l],
                           lam_init, tq, heads)
        o_lru, h1, c1 = _lru(pf, c0, h0, prm['lru_conv_w'][l], prm['lru_conv_b'][l], lw['wa'],
                             prm['lru_gate_a_b'][l], lw['wx'], prm['lru_gate_x_b'][l],
                             prm['lru_log_lambda'][l], tl_lru)
        o_gla, s1_t = _gla(pf, _state_to_bd(s0), lw['w2p'], prm['gla_gate_b'][l], lw['gn_t'], tl_gla)
        x, fc1 = _ffn(x, o_att, o_lru, o_gla, lw['w_out'], prm['norm_ffn_g'][l], fc0, lw['wu'], lw['wg'],
                      prm['ffn_conv_w'][l], prm['ffn_conv_b'][l], lw['wd'], prm['norm_final_g'],
                      bb, tl_ffn, tn, final_norm=(l == depth - 1))
        fc1 = fc1.transpose(0, 2, 1, 3).reshape(b, fc1.shape[2], f)
        aw = heads * ATT_VD
        outs.append(dict(k=pf[:, :, aw:2 * aw], v=pf[:, :, 2 * aw:3 * aw], kb=pb[:, :, aw:2 * aw],
                         vb=pb[:, :, 2 * aw:3 * aw], h=h1.reshape(b, width), c=c1,
                         s=_state_from_bd(s1_t, gla_heads), fc=fc1))
    return x, outs


def kernel(x_prompt, x_sample, cache_attn_k, cache_attn_v, state_lru_h, state_lru_conv, state_gla, state_ffn_conv, meta_tokens, norm_mix_g, w_in, attn_lambda, attn_subln_g, lru_conv_w, lru_conv_b, lru_gate_a_w, lru_gate_a_b, lru_gate_x_w, lru_gate_x_b, lru_log_lambda, gla_gate_w2, gla_gate_b, gla_norm_g, w_out, norm_ffn_g, ffn_w_up, ffn_conv_w, ffn_conv_b, ffn_w_gate, ffn_w_down, norm_final_g):
    depth = w_in.shape[0]
    b, seq, d = x_prompt.shape
    db, dseq, _ = x_sample.shape
    past = cache_attn_k.shape[2]
    heads = cache_attn_k.shape[3]
    gla_heads = state_gla.shape[2]
    n_meta = meta_tokens.shape[0]
    width = lru_conv_w.shape[-1]
    f = ffn_conv_w.shape[-1]
    att_w = heads * ATT_VD
    sizes = (att_w, width, gla_heads * GLA_DK, gla_heads * GLA_DV)

    prm = dict(norm_mix_g=norm_mix_g, attn_lambda=attn_lambda, attn_subln_g=attn_subln_g,
               lru_conv_w=lru_conv_w, lru_conv_b=lru_conv_b, lru_gate_a_b=lru_gate_a_b,
               lru_gate_x_b=lru_gate_x_b, lru_log_lambda=lru_log_lambda, gla_gate_b=gla_gate_b,
               norm_ffn_g=norm_ffn_g, ffn_conv_w=ffn_conv_w, ffn_conv_b=ffn_conv_b,
               norm_final_g=norm_final_g)
    layers = [_prep_layer(l, w_in, lru_gate_a_w, lru_gate_x_w, gla_gate_w2, gla_norm_g, w_out,
                          ffn_w_up, ffn_w_gate, ffn_w_down, sizes) for l in range(depth)]
    tn = 256 if f % 256 == 0 else 128

    xm = jnp.broadcast_to(meta_tokens.astype(F32)[None], (b, n_meta, d))
    tiles_m = (b * n_meta, n_meta, n_meta, n_meta, b, n_meta, tn)
    _, om = _trunk(xm, None, None, 0, layers, prm, heads, gla_heads, tiles_m)

    pad_rows = 128 - n_meta
    pre_f = [(jnp.pad(o['kb'], ((0, 0), (0, pad_rows), (0, 0))),
              jnp.pad(o['vb'], ((0, 0), (0, pad_rows), (0, 0)))) for o in om]
    st_f = [(o['h'], o['c'], o['s'], o['fc']) for o in om]
    tiles_f = (_pick_tile(b * seq, 1024), _pick_tile(seq, 256), _pick_tile(seq, 512),
               _pick_tile(seq, 256), 1, _pick_tile(seq, 512), tn)
    yp, of = _trunk(x_prompt, st_f, pre_f, n_meta, layers, prm, heads, gla_heads, tiles_f)

    pre_s = [(cache_attn_k[l].reshape(db, past, att_w), cache_attn_v[l].reshape(db, past, att_w))
             for l in range(depth)]
    st_s = [(state_lru_h[l], state_lru_conv[l], state_gla[l], state_ffn_conv[l]) for l in range(depth)]
    tiles_s = (_pick_tile(db * dseq, 1024), dseq, dseq, dseq, db, dseq, tn)
    ys, os_ = _trunk(x_sample, st_s, pre_s, past, layers, prm, heads, gla_heads, tiles_s)

    def stack_p(name, tail_shape):
        return jnp.stack([jnp.concatenate([m[name], fr[name]], axis=1).reshape((b, n_meta + seq) + tail_shape)
                          for m, fr in zip(om, of)])

    k_p = stack_p('k', (heads, ATT_VD))
    v_p = stack_p('v', (heads, ATT_VD))
    k_s = jnp.stack([o['k'].reshape(db, dseq, heads, ATT_VD) for o in os_])
    v_s = jnp.stack([o['v'].reshape(db, dseq, heads, ATT_VD) for o in os_])
    st = lambda outs, name: jnp.stack([o[name] for o in outs])
    return (yp, ys, k_p, v_p, st(of, 'h'), st(of, 'c'), st(of, 's'), st(of, 'fc'),
            k_s, v_s, st(os_, 'h'), st(os_, 'c'), st(os_, 's'), st(os_, 'fc'))
```

```python
import functools
import math

import jax
import jax.numpy as jnp
from jax import lax
from jax.experimental import pallas as pl
from jax.experimental.pallas import tpu as pltpu

F32 = jnp.float32
BF16 = jnp.bfloat16

CHUNK = 64
EPS = 1e-6
ATT_DH = 64
ATT_VD = 128
LRU_C = 8.0
GLA_DK = 32
GLA_DV = 64
GLA_RANK = 16
GLA_TAU = 16.0
GLA_SUB = 16

PROJ_TN = 512
PROJ_W = 3072
QKV_W = 1536
COL_LRU = 3
COL_GLA_A = 4
COL_GLA_B = 5

VMEM_LIMIT = 48 * 1024 * 1024


def _rms(x, g):
    return x * lax.rsqrt(jnp.mean(x * x, axis=-1, keepdims=True) + EPS) * g


def _params(*sem):
    return pltpu.CompilerParams(dimension_semantics=sem, vmem_limit_bytes=VMEM_LIMIT)


def _in_proj_kernel(x_ref, g_ref, w_ref, of_ref, ob_ref, hn_ref, *, n_bf_tiles, q_scale):
    j = pl.program_id(1)

    @pl.when(j == 0)
    def _():
        hn_ref[...] = _rms(x_ref[...], g_ref[...]).astype(BF16)

    y = jnp.dot(hn_ref[...], w_ref[...], preferred_element_type=F32)
    of_ref[...] = y

    @pl.when(j < n_bf_tiles)
    def _():
        scale = jnp.where(j == 0, q_scale, 1.0).astype(F32)
        ob_ref[...] = (y * scale).astype(BF16)


def _in_proj(x2, g, w, tm):
    m, d = x2.shape
    n_bf = QKV_W // PROJ_TN
    kern = functools.partial(_in_proj_kernel, n_bf_tiles=n_bf, q_scale=ATT_DH ** -0.5 * math.log2(math.e))
    return pl.pallas_call(
        kern,
        grid=(m // tm, PROJ_W // PROJ_TN),
        in_specs=[
            pl.BlockSpec((tm, d), lambda i, j: (i, 0)),
            pl.BlockSpec((1, d), lambda i, j: (0, 0)),
            pl.BlockSpec((d, PROJ_TN), lambda i, j: (0, j)),
        ],
        out_specs=[
            pl.BlockSpec((tm, PROJ_TN), lambda i, j: (i, j)),
            pl.BlockSpec((tm, PROJ_TN), lambda i, j: (i, jnp.minimum(j, n_bf - 1))),
        ],
        out_shape=[jax.ShapeDtypeStruct((m, PROJ_W), F32), jax.ShapeDtypeStruct((m, QKV_W), BF16)],
        scratch_shapes=[pltpu.VMEM((tm, d), BF16)],
        compiler_params=_params("parallel", "arbitrary"),
        name="in_proj",
    )(x2, g.reshape(1, d), w)


def _attn_kernel(*refs, tq, tk, n_pre, n_pre_valid, tkp, lam_init, has_pre):
    if has_pre:
        lam_ref, g_ref, q_ref, k_ref, v_ref, kp_ref, vp_ref, o_ref, m_ref, l_ref, acc_ref = refs
    else:
        lam_ref, g_ref, q_ref, k_ref, v_ref, o_ref, m_ref, l_ref, acc_ref = refs
    qi = pl.program_id(2)

    q = q_ref[0]
    lane = lax.broadcasted_iota(jnp.int32, q.shape, 1)
    zero = jnp.zeros_like(q)
    q2 = jnp.concatenate([jnp.where(lane < ATT_DH, q, zero), jnp.where(lane >= ATT_DH, q, zero)], axis=0)

    m_ref[...] = jnp.full(m_ref.shape, -jnp.inf, F32)
    l_ref[...] = jnp.zeros(l_ref.shape, F32)
    acc_ref[...] = jnp.zeros(acc_ref.shape, F32)

    def update(kt, vt, mask):
        s = lax.dot_general(q2, kt.astype(BF16), (((1,), (1,)), ((), ())), preferred_element_type=F32)
        if mask is not None:
            s = jnp.where(mask, s, -jnp.inf)
        m_old = m_ref[...]
        m_new = jnp.maximum(m_old, jnp.max(s, axis=-1, keepdims=True))
        alpha = jnp.exp2(m_old - m_new)
        p = jnp.exp2(s - m_new)
        l_ref[...] = alpha * l_ref[...] + jnp.sum(p, axis=-1, keepdims=True)
        m_ref[...] = m_new
        acc_ref[...] = alpha * acc_ref[...] + jnp.dot(p.astype(BF16), vt.astype(BF16),
                                                      preferred_element_type=F32)

    if has_pre:
        n_tiles = n_pre // tkp
        full_tiles = n_pre_valid // tkp
        if full_tiles > 0:
            def pre_body(t, c):
                off = pl.multiple_of(t * tkp, tkp)
                update(kp_ref[0, pl.ds(off, tkp), :], vp_ref[0, pl.ds(off, tkp), :], None)
                return c
            lax.fori_loop(0, full_tiles, pre_body, 0)
        if full_tiles < n_tiles:
            col = lax.broadcasted_iota(jnp.int32, (2 * tq, tkp), 1)
            off = full_tiles * tkp
            update(kp_ref[0, off:off + tkp, :], vp_ref[0, off:off + tkp, :],
                   col < (n_pre_valid - off))

    def own_body(t, c):
        off = pl.multiple_of(t * tk, tk)
        update(k_ref[0, pl.ds(off, tk), :], v_ref[0, pl.ds(off, tk), :], None)
        return c
    lax.fori_loop(0, qi * (tq // tk), own_body, 0)

    off = pl.multiple_of(qi * tq, tq)
    if tq > CHUNK:
        row = lax.broadcasted_iota(jnp.int32, (2 * tq, tq), 0)
        col = lax.broadcasted_iota(jnp.int32, (2 * tq, tq), 1)
        row = jnp.where(row >= tq, row - tq, row)
        mask = (col // CHUNK) <= (row // CHUNK)
    else:
        mask = None
    update(k_ref[0, pl.ds(off, tq), :], v_ref[0, pl.ds(off, tq), :], mask)

    lq = lam_ref[...]
    lam = (jnp.exp(jnp.sum(lq[0:1] * lq[1:2], axis=-1, keepdims=True))
           - jnp.exp(jnp.sum(lq[2:3] * lq[3:4], axis=-1, keepdims=True)) + lam_init)
    acc = acc_ref[...]
    l = l_ref[...]
    o = acc[:tq] / l[:tq] - lam * (acc[tq:] / l[tq:])
    o_ref[0] = _rms(o, g_ref[...]) * (1.0 - lam_init)


def _attention(qkv, pre_k, pre_v, n_pre_valid, lam_p, g, lam_init, tq, heads):
    b, L, _ = qkv.shape
    has_pre = pre_k is not None
    tk = tq
    kern_kw = dict(tq=tq, tk=tk, lam_init=lam_init, has_pre=has_pre, n_pre=0, n_pre_valid=0, tkp=0)
    in_specs = [
        pl.BlockSpec((4, ATT_DH), lambda bi, h, qi: (0, 0)),
        pl.BlockSpec((1, ATT_VD), lambda bi, h, qi: (0, 0)),
        pl.BlockSpec((1, tq, 128), lambda bi, h, qi: (bi, qi, h)),
        pl.BlockSpec((1, L, 128), lambda bi, h, qi: (bi, 0, heads + h)),
        pl.BlockSpec((1, L, 128), lambda bi, h, qi: (bi, 0, 2 * heads + h)),
    ]
    args = [lam_p, g.reshape(1, ATT_VD), qkv, qkv, qkv]
    if has_pre:
        lp = pre_k.shape[1]
        tkp = min(lp, 512)
        kern_kw.update(n_pre=lp, n_pre_valid=n_pre_valid, tkp=tkp)
        in_specs += [pl.BlockSpec((1, lp, 128), lambda bi, h, qi: (bi, 0, h)),
                     pl.BlockSpec((1, lp, 128), lambda bi, h, qi: (bi, 0, h))]
        args += [pre_k, pre_v]
    return pl.pallas_call(
        functools.partial(_attn_kernel, **kern_kw),
        grid=(b, heads, L // tq),
        in_specs=in_specs,
        out_specs=pl.BlockSpec((1, tq, 128), lambda bi, h, qi: (bi, qi, h)),
        out_shape=jax.ShapeDtypeStruct((b, L, heads * ATT_VD), F32),
        scratch_shapes=[pltpu.VMEM((2 * tq, 1), F32), pltpu.VMEM((2 * tq, 1), F32),
                        pltpu.VMEM((2 * tq, ATT_VD), F32)],
        compiler_params=_params("parallel", "parallel", "arbitrary"),
        name="diff_attn",
    )(*args)


def _attn_long_kernel(lam_ref, g_ref, q_ref, k_ref, vt_ref, kp_ref, vtp_ref, o_ref,
                      sa_ref, sb_ref, m_ref, acc_ref, *, tq, tk, n_pre_valid, lam_init):
    qi = pl.program_id(2)
    nt = (((1,), (1,)), ((), ()))

    q = q_ref[0]
    r = lax.broadcasted_iota(jnp.int32, (128, 128), 0)
    c = lax.broadcasted_iota(jnp.int32, (128, 128), 1)
    e1 = jnp.where((r == c) & (r < ATT_DH), 1.0, 0.0).astype(BF16)
    e2 = jnp.where((r == c) & (r >= ATT_DH), 1.0, 0.0).astype(BF16)
    q2t = jnp.concatenate([lax.dot_general(e1, q, nt, preferred_element_type=F32),
                           lax.dot_general(e2, q, nt, preferred_element_type=F32)], axis=1).astype(BF16)

    m_ref[...] = jnp.full(m_ref.shape, -jnp.inf, F32)
    acc_ref[...] = jnp.zeros(acc_ref.shape, F32)

    def scores(kt):
        return jnp.dot(kt, q2t, preferred_element_type=F32)

    def absorb(s, vt, mask):
        if mask is not None:
            s = jnp.where(mask, s, -jnp.inf)
        m_old = m_ref[...]
        m_new = jnp.maximum(m_old, jnp.max(s, axis=0, keepdims=True))
        alpha = jnp.exp2(m_old - m_new)
        p = jnp.exp2(s - m_new).astype(BF16)
        m_ref[...] = m_new
        acc_ref[...] = alpha * acc_ref[...] + jnp.dot(vt, p, preferred_element_type=F32)

    def step(cur_ref, nxt_ref, t):
        off = pl.multiple_of((t + 1) * tk, tk)
        nxt_ref[...] = scores(k_ref[0, pl.ds(off, tk), :])
        absorb(cur_ref[...], vt_ref[0, 0, t], None)

    n_pre = kp_ref.shape[1]
    key = lax.broadcasted_iota(jnp.int32, (n_pre, 2 * tq), 0)
    absorb(scores(kp_ref[0]), vtp_ref[0, 0], key < n_pre_valid)

    sa_ref[...] = scores(k_ref[0, 0:tk, :])

    def pair(u, carry):
        step(sa_ref, sb_ref, 2 * u)
        step(sb_ref, sa_ref, 2 * u + 1)
        return carry
    lax.fori_loop(0, qi // 2, pair, 0)

    key = lax.broadcasted_iota(jnp.int32, (tk, 2 * tq), 0)
    qry = lax.broadcasted_iota(jnp.int32, (tk, 2 * tq), 1)
    qry = jnp.where(qry >= tq, qry - tq, qry)
    shift = CHUNK.bit_length() - 1
    mask = (key >> shift) <= (qry >> shift)
    odd = qi % 2 == 1

    @pl.when(odd)
    def _():
        step(sa_ref, sb_ref, qi - 1)
        absorb(sb_ref[...], vt_ref[0, 0, qi], mask)

    @pl.when(jnp.logical_not(odd))
    def _():
        absorb(sa_ref[...], vt_ref[0, 0, qi], mask)

    lq = lam_ref[...]
    lam = (jnp.exp(jnp.sum(lq[0:1] * lq[1:2], axis=-1, keepdims=True))
           - jnp.exp(jnp.sum(lq[2:3] * lq[3:4], axis=-1, keepdims=True)) + lam_init)
    acc = acc_ref[...]
    num = acc[:ATT_VD]
    den = acc[ATT_VD:ATT_VD + 1]
    o_t = num[:, :tq] / den[:, :tq] - lam * (num[:, tq:] / den[:, tq:])
    o_ref[0] = _rms(o_t.T, g_ref[...]) * (1.0 - lam_init)


VT_ROWS = ATT_VD + 16


def _values_t(v):
    ones = jnp.ones(v.shape[:-1] + (1,), v.dtype)
    zeros = jnp.zeros(v.shape[:-1] + (VT_ROWS - ATT_VD - 1,), v.dtype)
    return jnp.swapaxes(jnp.concatenate([v, ones, zeros], axis=-1), -1, -2)


def _attention_long(qkv, pre_k, pre_vt, n_pre_valid, lam_p, g, lam_init, tq, heads):
    b, L, _ = qkv.shape
    tk = tq
    nt = L // tk
    lp = pre_k.shape[1]
    v = qkv[:, :, 2 * heads * 128:].reshape(b, nt, tk, heads, ATT_VD)
    vt = _values_t(v.transpose(0, 3, 1, 2, 4))
    kern = functools.partial(_attn_long_kernel, tq=tq, tk=tk, n_pre_valid=n_pre_valid, lam_init=lam_init)
    return pl.pallas_call(
        kern,
        grid=(b, heads, L // tq),
        in_specs=[
            pl.BlockSpec((4, ATT_DH), lambda bi, h, qi: (0, 0)),
            pl.BlockSpec((1, ATT_VD), lambda bi, h, qi: (0, 0)),
            pl.BlockSpec((1, tq, 128), lambda bi, h, qi: (bi, qi, h)),
            pl.BlockSpec((1, L, 128), lambda bi, h, qi: (bi, 0, heads + h)),
            pl.BlockSpec((1, 1, nt, VT_ROWS, tk), lambda bi, h, qi: (bi, h, 0, 0, 0)),
            pl.BlockSpec((1, lp, 128), lambda bi, h, qi: (bi, 0, h)),
            pl.BlockSpec((1, 1, VT_ROWS, lp), lambda bi, h, qi: (bi, h, 0, 0)),
        ],
        out_specs=pl.BlockSpec((1, tq, 128), lambda bi, h, qi: (bi, qi, h)),
        out_shape=jax.ShapeDtypeStruct((b, L, heads * ATT_VD), F32),
        scratch_shapes=[pltpu.VMEM((tk, 2 * tq), F32), pltpu.VMEM((tk, 2 * tq), F32),
                        pltpu.VMEM((1, 2 * tq), F32), pltpu.VMEM((VT_ROWS, 2 * tq), F32)],
        compiler_params=_params("parallel", "parallel", "arbitrary"),
        name="diff_attn_long",
    )(lam_p, g.reshape(1, ATT_VD), qkv, qkv, vt, pre_k, pre_vt)


def _lru_kernel(p_ref, c0_ref, h0_ref, cw_ref, cb_ref, wa_ref, ba_ref, wx_ref, bx_ref, ll_ref,
                o_ref, hout_ref, cout_ref, ext_ref, hc_ref, *, tl, width, conv):
    li = pl.program_id(1)
    pad = 8

    @pl.when(li == 0)
    def _():
        ext_ref[pad - (conv - 1):pad] = c0_ref[0]
        hc_ref[...] = h0_ref[0]

    lx = p_ref[0, :, 0:width]
    lg = p_ref[0, :, width:2 * width]
    ext_ref[pad:pad + tl] = lx
    cw = cw_ref[...]
    xc = cb_ref[...] + cw[conv - 1:conv] * lx
    for s in range(1, conv):
        xc = xc + cw[conv - 1 - s:conv - s] * ext_ref[pad - s:pad - s + tl]
    tail = ext_ref[pad + tl - (conv - 1):pad + tl]
    cout_ref[0] = tail
    ext_ref[pad - (conv - 1):pad] = tail

    xb = xc.astype(BF16)
    r = jax.nn.sigmoid(jnp.dot(xb, wa_ref[...], preferred_element_type=F32) + ba_ref[...])
    i = jax.nn.sigmoid(jnp.dot(xb, wx_ref[...], preferred_element_type=F32) + bx_ref[...])
    z = -ll_ref[...]
    softplus = jnp.maximum(z, 0.0) + jnp.log1p(jnp.exp(-jnp.abs(z)))
    log_a = -LRU_C * r * softplus
    a = jnp.exp(log_a)
    u = jnp.sqrt(-jnp.tanh(log_a) * (1.0 + a * a)) * i * xc

    row = lax.broadcasted_iota(jnp.int32, a.shape, 0)
    d = 1
    while d < tl:
        keep = row >= d
        a_prev = jnp.where(keep, pltpu.roll(a, d, 0), 1.0)
        u_prev = jnp.where(keep, pltpu.roll(u, d, 0), 0.0)
        u = u + a * u_prev
        a = a * a_prev
        d *= 2
    h = u + a * hc_ref[...]
    hc_ref[...] = h[tl - 1:tl]
    hout_ref[0] = h[tl - 1:tl]
    o_ref[0] = h * jax.nn.gelu(lg)


def _lru(proj, c0, h0, cw, cb, wa, ba, wx, bx, ll, tl):
    b, L, _ = proj.shape
    conv, width = cw.shape
    vec = lambda a: a.reshape(1, width)
    const = lambda shape: pl.BlockSpec(shape, lambda bi, li: (0,) * len(shape))
    kern = functools.partial(_lru_kernel, tl=tl, width=width, conv=conv)
    return pl.pallas_call(
        kern,
        grid=(b, L // tl),
        in_specs=[
            pl.BlockSpec((1, tl, 2 * width), lambda bi, li: (bi, li, COL_LRU)),
            pl.BlockSpec((1, conv - 1, width), lambda bi, li: (bi, 0, 0)),
            pl.BlockSpec((1, 1, width), lambda bi, li: (bi, 0, 0)),
            const((conv, width)), const((1, width)),
            const((width, width)), const((1, width)),
            const((width, width)), const((1, width)), const((1, width)),
        ],
        out_specs=[
            pl.BlockSpec((1, tl, width), lambda bi, li: (bi, li, 0)),
            pl.BlockSpec((1, 1, width), lambda bi, li: (bi, 0, 0)),
            pl.BlockSpec((1, conv - 1, width), lambda bi, li: (bi, 0, 0)),
        ],
        out_shape=[jax.ShapeDtypeStruct((b, L, width), F32),
                   jax.ShapeDtypeStruct((b, 1, width), F32),
                   jax.ShapeDtypeStruct((b, conv - 1, width), F32)],
        scratch_shapes=[pltpu.VMEM((tl + 8, width), F32), pltpu.VMEM((1, width), F32)],
        compiler_params=_params("parallel", "arbitrary"),
        name="rg_lru",
    )(proj, c0, h0.reshape(b, 1, width), cw, vec(cb), wa, vec(ba), wx, vec(bx), vec(ll))


def _gla_kernel(pa_ref, pb_ref, s0_ref, w2_ref, gb_ref, gn_ref, o_ref, sout_ref, st_ref, oi_ref,
                *, tl, chunk, kw, vw):
    li = pl.program_id(1)

    @pl.when(li == 0)
    def _():
        st_ref[...] = s0_ref[0]

    q = pa_ref[0, :, 0:kw] * (GLA_DK ** -0.5)
    k = pa_ref[0, :, kw:2 * kw]
    v = pa_ref[0, :, 2 * kw:2 * kw + vw]
    gog = pb_ref[0, :, 0:vw]
    glr = pb_ref[0, :, vw:2 * vw]

    x = jnp.dot(glr.astype(BF16), w2_ref[...], preferred_element_type=F32) + gb_ref[...]
    log_g = (jnp.minimum(x, 0.0) - jnp.log1p(jnp.exp(-jnp.abs(x)))) / GLA_TAU

    rowc = lax.broadcasted_iota(jnp.int32, (tl, kw), 0) % chunk
    bc = log_g
    d = 1
    while d < chunk:
        bc = bc + jnp.where(rowc >= d, pltpu.roll(bc, d, 0), 0.0)
        d *= 2

    hk = lax.broadcasted_iota(jnp.int32, (kw, vw), 0) // GLA_DK
    hv = lax.broadcasted_iota(jnp.int32, (kw, vw), 1) // GLA_DV
    spread = jnp.where(hk == hv, 1.0, 0.0).astype(BF16)

    sub = min(GLA_SUB, chunk)
    row_s = rowc % sub
    o_band = jnp.dot((q * k).astype(BF16), spread, preferred_element_type=F32) * v
    for dl in range(1, sub):
        valid = row_s >= dl
        rel = jnp.where(valid, bc - pltpu.roll(bc, dl, 0), -jnp.inf)
        w = q * pltpu.roll(k, dl, 0) * jnp.exp(rel)
        att = jnp.dot(w.astype(BF16), spread, preferred_element_type=F32)
        o_band = o_band + att * pltpu.roll(v, dl, 0)
    oi_ref[...] = o_band

    n_sub = chunk // sub
    nt_dims = (((1,), (1,)), ((), ()))
    hs = lax.broadcasted_iota(jnp.int32, (vw, kw), 0) // GLA_DV
    hl = lax.broadcasted_iota(jnp.int32, (vw, kw), 1) // GLA_DK
    same_head = hs == hl
    heads = kw // GLA_DK
    if n_sub > 1:
        cat = (n_sub - 1) * kw
        kr = lax.broadcasted_iota(jnp.int32, (heads * chunk, cat), 0) // chunk
        kl = (lax.broadcasted_iota(jnp.int32, (heads * chunk, cat), 1) % kw) // GLA_DK
        key_head = kr == kl
        vr = lax.broadcasted_iota(jnp.int32, (heads * chunk, vw), 0) // chunk
        vl = lax.broadcasted_iota(jnp.int32, (heads * chunk, vw), 1) // GLA_DV
        val_head = vr == vl
        ri = lax.broadcasted_iota(jnp.int32, (chunk, kw), 0)
    for c in range(tl // chunk):
        rows = slice(c * chunk, (c + 1) * chunk)
        b_c = bc[rows]
        q_c = q[rows]
        k_c = k[rows]
        v_c = v[rows].astype(BF16)
        o_c = jnp.zeros((chunk, vw), F32)
        if n_sub > 1:
            qs, ks = [], []
            for i in range(1, n_sub):
                r_i = b_c[i * sub - 1:i * sub]
                in_q = (ri >= i * sub) & (ri < (i + 1) * sub)
                in_k = ri < i * sub
                qs.append(jnp.where(in_q, q_c * jnp.exp(jnp.where(in_q, b_c - r_i, 0.0)), 0.0))
                ks.append(jnp.where(in_k, k_c * jnp.exp(jnp.where(in_k, r_i - b_c, 0.0)), 0.0))
            q_cat = jnp.concatenate(qs, axis=1).astype(BF16)
            k_cat = jnp.concatenate(ks, axis=1).astype(BF16)
            k_bd = jnp.where(key_head, jnp.concatenate([k_cat] * heads, axis=0), jnp.zeros_like(k_cat[:1]))
            att = lax.dot_general(q_cat, k_bd, nt_dims, preferred_element_type=F32)
            v_bd = jnp.where(val_head, jnp.concatenate([v_c] * heads, axis=0), jnp.zeros_like(v_c[:1]))
            o_c = jnp.dot(att.astype(BF16), v_bd, preferred_element_type=F32)
        b_last = b_c[chunk - 1:chunk]
        st = st_ref[...]
        qe = (q_c * jnp.exp(b_c)).astype(BF16)
        o_c = o_c + lax.dot_general(qe, st.astype(BF16), nt_dims, preferred_element_type=F32)
        ke = (k_c * jnp.exp(b_last - b_c)).astype(BF16)
        kv_t = lax.dot_general(v_c, ke, (((0,), (0,)), ((), ())), preferred_element_type=F32)
        st_ref[...] = jnp.exp(b_last) * st + jnp.where(same_head, kv_t, 0.0)
        oi_ref[rows, :] += o_c
    sout_ref[0] = st_ref[...]

    o = oi_ref[...]
    o2 = o * o
    lane_h = lax.broadcasted_iota(jnp.int32, o.shape, 1) // GLA_DV
    ms = jnp.zeros_like(o)
    for h in range(vw // GLA_DV):
        in_h = lane_h == h
        s_h = jnp.sum(jnp.where(in_h, o2, 0.0), axis=-1, keepdims=True) * (1.0 / GLA_DV)
        ms = jnp.where(in_h, s_h, ms)
    o_ref[0] = o * lax.rsqrt(ms + EPS) * gn_ref[...] * jax.nn.silu(gog)


def _gla(proj, s0_t, w2p, gb, gn_t, tl):
    b, L, _ = proj.shape
    vw, kw = s0_t.shape[1:]
    chunk = min(CHUNK, L)
    const = lambda shape: pl.BlockSpec(shape, lambda bi, li: (0,) * len(shape))
    kern = functools.partial(_gla_kernel, tl=tl, chunk=chunk, kw=kw, vw=vw)
    return pl.pallas_call(
        kern,
        grid=(b, L // tl),
        in_specs=[
            pl.BlockSpec((1, tl, PROJ_TN), lambda bi, li: (bi, li, COL_GLA_A)),
            pl.BlockSpec((1, tl, PROJ_TN), lambda bi, li: (bi, li, COL_GLA_B)),
            pl.BlockSpec((1, vw, kw), lambda bi, li: (bi, 0, 0)),
            const((vw, kw)), const((1, kw)), const((1, vw)),
        ],
        out_specs=[
            pl.BlockSpec((1, tl, vw), lambda bi, li: (bi, li, 0)),
            pl.BlockSpec((1, vw, kw), lambda bi, li: (bi, 0, 0)),
        ],
        out_shape=[jax.ShapeDtypeStruct((b, L, vw), F32), jax.ShapeDtypeStruct((b, vw, kw), F32)],
        scratch_shapes=[pltpu.VMEM((vw, kw), F32), pltpu.VMEM((tl, vw), F32)],
        compiler_params=_params("parallel", "arbitrary"),
        name="gla",
    )(proj, proj, s0_t, w2p, gb.reshape(1, kw), gn_t.reshape(1, vw))


def _ffn_kernel(x_ref, oa_ref, ol_ref, og_ref, wo_ref, g_ref, c0_ref, wu_ref, wg_ref, cw_ref, cb_ref,
                wd_ref, gf_ref, out_ref, cout_ref, acc_ref, hn_ref, carry_ref,
                *, bb, tl, tn, conv, final_norm):
    li = pl.program_id(1)
    j = pl.program_id(2)
    m = bb * tl
    d = x_ref.shape[-1]
    wa = oa_ref.shape[-1]
    wl = ol_ref.shape[-1]

    @pl.when(j == 0)
    def _():
        mix = jnp.dot(oa_ref[...].reshape(m, wa).astype(BF16), wo_ref[0:wa], preferred_element_type=F32)
        mix += jnp.dot(ol_ref[...].reshape(m, wl).astype(BF16), wo_ref[wa:wa + wl],
                       preferred_element_type=F32)
        mix += jnp.dot(og_ref[...].reshape(m, d - wa - wl).astype(BF16), wo_ref[wa + wl:d],
                       preferred_element_type=F32)
        xn = x_ref[...].reshape(m, d) + mix
        acc_ref[...] = xn
        hn_ref[...] = _rms(xn, g_ref[...]).astype(BF16)

    @pl.when(li == 0)
    def _():
        carry_ref[j] = c0_ref[...]

    hn = hn_ref[...]
    u = jnp.dot(hn, wu_ref[...], preferred_element_type=F32)
    gate = jnp.dot(hn, wg_ref[...], preferred_element_type=F32)

    prev = carry_ref[j]
    t = lax.broadcasted_iota(jnp.int32, (bb, tl, tn), 1)
    cw = cw_ref[...]
    u3 = u.reshape(bb, tl, tn)
    uc = cb_ref[...] + cw[conv - 1:conv] * u3
    for s in range(1, conv):
        shifted = pltpu.roll(u, s, 0).reshape(bb, tl, tn)
        for r in range(s):
            shifted = jnp.where(t == r, prev[:, conv - 1 - s + r:conv - s + r, :], shifted)
        uc = uc + cw[conv - 1 - s:conv - s] * shifted
    tail = u3[:, tl - (conv - 1):tl, :]
    carry_ref[j] = tail
    cout_ref[:, j] = tail

    act = (jax.nn.gelu(uc) * gate.reshape(bb, tl, tn)).reshape(m, tn).astype(BF16)
    acc_ref[...] += jnp.dot(act, wd_ref[...], preferred_element_type=F32)

    @pl.when(j == pl.num_programs(2) - 1)
    def _():
        y = acc_ref[...]
        if final_norm:
            y = _rms(y, gf_ref[...])
        out_ref[...] = y.reshape(bb, tl, d)


def _ffn(x, oa, ol, og, wo, g, c0, wu, wg, cw, cb, wd, gf, bb, tl, tn, final_norm):
    b, L, d = x.shape
    conv, f = cw.shape
    nj = f // tn
    kern = functools.partial(_ffn_kernel, bb=bb, tl=tl, tn=tn, conv=conv, final_norm=final_norm)
    act_spec = lambda w: pl.BlockSpec((bb, tl, w), lambda bi, li, j: (bi, li, 0))
    const = lambda shape: pl.BlockSpec(shape, lambda bi, li, j: (0,) * len(shape))
    return pl.pallas_call(
        kern,
        grid=(b // bb, L // tl, nj),
        in_specs=[
            act_spec(d), act_spec(oa.shape[-1]), act_spec(ol.shape[-1]), act_spec(og.shape[-1]),
            const((d, d)), const((1, d)),
            pl.BlockSpec((bb, conv - 1, tn), lambda bi, li, j: (bi, 0, j)),
            pl.BlockSpec((d, tn), lambda bi, li, j: (0, j)),
            pl.BlockSpec((d, tn), lambda bi, li, j: (0, j)),
            pl.BlockSpec((conv, tn), lambda bi, li, j: (0, j)),
            pl.BlockSpec((1, tn), lambda bi, li, j: (0, j)),
            pl.BlockSpec((tn, d), lambda bi, li, j: (j, 0)),
            const((1, d)),
        ],
        out_specs=[
            pl.BlockSpec((bb, tl, d), lambda bi, li, j: (bi, li, 0)),
            pl.BlockSpec((bb, nj, conv - 1, tn), lambda bi, li, j: (bi, 0, 0, 0)),
        ],
        out_shape=[jax.ShapeDtypeStruct((b, L, d), F32), jax.ShapeDtypeStruct((b, nj, conv - 1, tn), F32)],
        scratch_shapes=[pltpu.VMEM((bb * tl, d), F32), pltpu.VMEM((bb * tl, d), BF16),
                        pltpu.VMEM((nj, bb, conv - 1, tn), F32)],
        compiler_params=_params("parallel", "arbitrary", "arbitrary"),
        name="mix_ffn",
    )(x, oa, ol, og, wo, g.reshape(1, d), c0, wu, wg, cw, cb.reshape(1, f), wd, gf.reshape(1, d))


def _block_diag(w):
    n, c, dd = w.shape
    eye = jnp.eye(n, dtype=w.dtype)
    return jnp.einsum('ncd,nm->ncmd', w, eye).reshape(n * c, n * dd)


def _prep_layer(l, w_in, lru_gate_a_w, lru_gate_x_w, gla_gate_w2, gla_norm_g, w_out, ffn_w_up,
                ffn_w_gate, ffn_w_down, sizes):
    att_w, lru_w, gla_kw, gla_vw = sizes
    o = 0
    parts = {}
    for name, wdt in (("q", att_w), ("k", att_w), ("v", att_w), ("lx", lru_w), ("lg", lru_w),
                      ("gq", gla_kw), ("gk", gla_kw), ("gv", gla_vw), ("glr", GLA_RANK), ("gog", gla_vw)):
        parts[name] = w_in[l][:, o:o + wdt]
        o += wdt
    d = w_in.shape[1]
    pad = jnp.zeros((d, PROJ_W - o), w_in.dtype)
    order = ("q", "k", "v", "lx", "lg", "gq", "gk", "gv", "gog", "glr")
    w_in_p = jnp.concatenate([parts[n] for n in order] + [pad], axis=1).astype(BF16)
    w2p = jnp.zeros((gla_vw, gla_kw), F32).at[:GLA_RANK].set(gla_gate_w2[l]).astype(BF16)
    return dict(
        w_in=w_in_p,
        wa=_block_diag(lru_gate_a_w[l]).astype(BF16),
        wx=_block_diag(lru_gate_x_w[l]).astype(BF16),
        w2p=w2p,
        gn_t=jnp.tile(gla_norm_g[l], gla_vw // GLA_DV),
        w_out=w_out[l].astype(BF16),
        wu=ffn_w_up[l].astype(BF16), wg=ffn_w_gate[l].astype(BF16), wd=ffn_w_down[l].astype(BF16),
    )


def _state_to_bd(s):
    b, h, dk, dv = s.shape
    eye = jnp.eye(h, dtype=s.dtype)
    return jnp.einsum('bhde,hg->bhegd', s, eye).reshape(b, h * dv, h * dk)


def _state_from_bd(st, h):
    b, vw, kw = st.shape
    dv, dk = vw // h, kw // h
    s5 = st.reshape(b, h, dv, h, dk)
    diag = jnp.stack([s5[:, i, :, i, :] for i in range(h)], axis=1)
    return diag.transpose(0, 1, 3, 2)


def _pick_tile(n, target):
    t = min(n, target)
    while n % t:
        t //= 2
    return t


def _trunk(x, states, pre_kv, n_pre_valid, layers, prm, heads, gla_heads, tiles, long_attn=False):
    b, L, d = x.shape
    depth = len(layers)
    tm, tq, tl_lru, tl_gla, bb, tl_ffn, tn = tiles
    outs = []
    for l in range(depth):
        lw = layers[l]
        width = prm['lru_conv_w'].shape[-1]
        f = prm['ffn_conv_w'].shape[-1]
        if states is None:
            h0 = jnp.zeros((b, width), F32)
            c0 = jnp.zeros((b, prm['lru_conv_w'].shape[1] - 1, width), F32)
            s0 = jnp.zeros((b, gla_heads, GLA_DK, GLA_DV), F32)
            fc0 = jnp.zeros((b, prm['ffn_conv_w'].shape[1] - 1, f), F32)
        else:
            h0, c0, s0, fc0 = states[l]
        pf, pb = _in_proj(x.reshape(b * L, d), prm['norm_mix_g'][l], lw['w_in'], tm)
        pf = pf.reshape(b, L, PROJ_W)
        pb = pb.reshape(b, L, QKV_W)
        lam_init = 0.8 - 0.6 * math.exp(-0.3 * l)
        pk, pv = pre_kv[l] if pre_kv is not None else (None, None)
        attend = _attention_long if long_attn else _attention
        o_att = attend(pb, pk, pv, n_pre_valid, prm['attn_lambda'][l], prm['attn_subln_g'][l],
                       lam_init, tq, heads)
        o_lru, h1, c1 = _lru(pf, c0, h0, prm['lru_conv_w'][l], prm['lru_conv_b'][l], lw['wa'],
                             prm['lru_gate_a_b'][l], lw['wx'], prm['lru_gate_x_b'][l],
                             prm['lru_log_lambda'][l], tl_lru)
        o_gla, s1_t = _gla(pf, _state_to_bd(s0), lw['w2p'], prm['gla_gate_b'][l], lw['gn_t'], tl_gla)
        x, fc1 = _ffn(x, o_att, o_lru, o_gla, lw['w_out'], prm['norm_ffn_g'][l], fc0, lw['wu'], lw['wg'],
                      prm['ffn_conv_w'][l], prm['ffn_conv_b'][l], lw['wd'], prm['norm_final_g'],
                      bb, tl_ffn, tn, final_norm=(l == depth - 1))
        fc1 = fc1.transpose(0, 2, 1, 3).reshape(b, fc1.shape[2], f)
        aw = heads * ATT_VD
        outs.append(dict(k=pf[:, :, aw:2 * aw], v=pf[:, :, 2 * aw:3 * aw], kb=pb[:, :, aw:2 * aw],
                         vb=pb[:, :, 2 * aw:3 * aw], h=h1.reshape(b, width), c=c1,
                         s=_state_from_bd(s1_t, gla_heads), fc=fc1))
    return x, outs


def kernel(x_prompt, x_sample, cache_attn_k, cache_attn_v, state_lru_h, state_lru_conv, state_gla, state_ffn_conv, meta_tokens, norm_mix_g, w_in, attn_lambda, attn_subln_g, lru_conv_w, lru_conv_b, lru_gate_a_w, lru_gate_a_b, lru_gate_x_w, lru_gate_x_b, lru_log_lambda, gla_gate_w2, gla_gate_b, gla_norm_g, w_out, norm_ffn_g, ffn_w_up, ffn_conv_w, ffn_conv_b, ffn_w_gate, ffn_w_down, norm_final_g):
    depth = w_in.shape[0]
    b, seq, d = x_prompt.shape
    db, dseq, _ = x_sample.shape
    past = cache_attn_k.shape[2]
    heads = cache_attn_k.shape[3]
    gla_heads = state_gla.shape[2]
    n_meta = meta_tokens.shape[0]
    width = lru_conv_w.shape[-1]
    f = ffn_conv_w.shape[-1]
    att_w = heads * ATT_VD
    sizes = (att_w, width, gla_heads * GLA_DK, gla_heads * GLA_DV)

    prm = dict(norm_mix_g=norm_mix_g, attn_lambda=attn_lambda, attn_subln_g=attn_subln_g,
               lru_conv_w=lru_conv_w, lru_conv_b=lru_conv_b, lru_gate_a_b=lru_gate_a_b,
               lru_gate_x_b=lru_gate_x_b, lru_log_lambda=lru_log_lambda, gla_gate_b=gla_gate_b,
               norm_ffn_g=norm_ffn_g, ffn_conv_w=ffn_conv_w, ffn_conv_b=ffn_conv_b,
               norm_final_g=norm_final_g)
    layers = [_prep_layer(l, w_in, lru_gate_a_w, lru_gate_x_w, gla_gate_w2, gla_norm_g, w_out,
                          ffn_w_up, ffn_w_gate, ffn_w_down, sizes) for l in range(depth)]
    tn = 256 if f % 256 == 0 else 128

    xm = jnp.broadcast_to(meta_tokens.astype(F32)[None], (b, n_meta, d))
    tiles_m = (b * n_meta, n_meta, n_meta, n_meta, b, n_meta, tn)
    _, om = _trunk(xm, None, None, 0, layers, prm, heads, gla_heads, tiles_m)

    pad_rows = 128 - n_meta
    pre_f = [(jnp.pad(o['kb'], ((0, 0), (0, pad_rows), (0, 0))),
              _values_t(jnp.pad(o['vb'], ((0, 0), (0, pad_rows), (0, 0)))
                        .reshape(b, 128, heads, ATT_VD).transpose(0, 2, 1, 3))) for o in om]
    st_f = [(o['h'], o['c'], o['s'], o['fc']) for o in om]
    tiles_f = (_pick_tile(b * seq, 1024), _pick_tile(seq, 512), _pick_tile(seq, 512),
               _pick_tile(seq, 256), 1, _pick_tile(seq, 512), tn)
    yp, of = _trunk(x_prompt, st_f, pre_f, n_meta, layers, prm, heads, gla_heads, tiles_f,
                    long_attn=True)

    pre_s = [(cache_attn_k[l].reshape(db, past, att_w), cache_attn_v[l].reshape(db, past, att_w))
             for l in range(depth)]
    st_s = [(state_lru_h[l], state_lru_conv[l], state_gla[l], state_ffn_conv[l]) for l in range(depth)]
    tiles_s = (_pick_tile(db * dseq, 1024), dseq, dseq, dseq, db, dseq, tn)
    ys, os_ = _trunk(x_sample, st_s, pre_s, past, layers, prm, heads, gla_heads, tiles_s)

    def stack_p(name, tail_shape):
        return jnp.stack([jnp.concatenate([m[name], fr[name]], axis=1).reshape((b, n_meta + seq) + tail_shape)
                          for m, fr in zip(om, of)])

    k_p = stack_p('k', (heads, ATT_VD))
    v_p = stack_p('v', (heads, ATT_VD))
    k_s = jnp.stack([o['k'].reshape(db, dseq, heads, ATT_VD) for o in os_])
    v_s = jnp.stack([o['v'].reshape(db, dseq, heads, ATT_VD) for o in os_])
    st = lambda outs, name: jnp.stack([o[name] for o in outs])
    return (yp, ys, k_p, v_p, st(of, 'h'), st(of, 'c'), st(of, 's'), st(of, 'fc'),
            k_s, v_s, st(os_, 'h'), st(os_, 'c'), st(os_, 's'), st(os_, 'fc'))
```

```python
import functools
import math

import jax
import jax.numpy as jnp
from jax import lax
from jax.experimental import pallas as pl
from jax.experimental.pallas import tpu as pltpu

F32 = jnp.float32
BF16 = jnp.bfloat16

CHUNK = 64
EPS = 1e-6
ATT_DH = 64
ATT_VD = 128
LRU_C = 8.0
GLA_DK = 32
GLA_DV = 64
GLA_RANK = 16
GLA_TAU = 16.0
GLA_SUB = 16
FFN_PIECE = 256

PROJ_TN = 512
PROJ_W = 3072
QKV_W = 1536
MIX_W = PROJ_W - QKV_W
COL_LRU = 0
COL_GLA_A = 1
COL_GLA_B = 2

VMEM_LIMIT = 48 * 1024 * 1024


def _rms(x, g):
    return x * lax.rsqrt(jnp.mean(x * x, axis=-1, keepdims=True) + EPS) * g


def _params(*sem):
    return pltpu.CompilerParams(dimension_semantics=sem, vmem_limit_bytes=VMEM_LIMIT)


def _in_proj_kernel(x_ref, g_ref, w_ref, k_ref, v_ref, mix_ref, ob_ref, hn_ref, *, n_bf_tiles, q_scale):
    j = pl.program_id(1)

    @pl.when(j == 0)
    def _():
        hn_ref[...] = _rms(x_ref[...], g_ref[...]).astype(BF16)

    y = jnp.dot(hn_ref[...], w_ref[...], preferred_element_type=F32)

    @pl.when(j == 1)
    def _():
        k_ref[...] = y

    @pl.when(j == 2)
    def _():
        v_ref[...] = y

    @pl.when(j >= n_bf_tiles)
    def _():
        mix_ref[...] = y

    @pl.when(j < n_bf_tiles)
    def _():
        scale = jnp.where(j == 0, q_scale, 1.0).astype(F32)
        ob_ref[...] = (y * scale).astype(BF16)


def _in_proj(x2, g, w, tm):
    m, d = x2.shape
    n_bf = QKV_W // PROJ_TN
    kern = functools.partial(_in_proj_kernel, n_bf_tiles=n_bf, q_scale=ATT_DH ** -0.5 * math.log2(math.e))
    return pl.pallas_call(
        kern,
        grid=(m // tm, PROJ_W // PROJ_TN),
        in_specs=[
            pl.BlockSpec((tm, d), lambda i, j: (i, 0)),
            pl.BlockSpec((1, d), lambda i, j: (0, 0)),
            pl.BlockSpec((d, PROJ_TN), lambda i, j: (0, j)),
        ],
        out_specs=[
            pl.BlockSpec((tm, PROJ_TN), lambda i, j: (i, 0)),
            pl.BlockSpec((tm, PROJ_TN), lambda i, j: (i, 0)),
            pl.BlockSpec((tm, PROJ_TN), lambda i, j: (i, jnp.maximum(j - n_bf, 0))),
            pl.BlockSpec((tm, PROJ_TN), lambda i, j: (i, jnp.minimum(j, n_bf - 1))),
        ],
        out_shape=[jax.ShapeDtypeStruct((m, PROJ_TN), F32), jax.ShapeDtypeStruct((m, PROJ_TN), F32),
                   jax.ShapeDtypeStruct((m, MIX_W), F32), jax.ShapeDtypeStruct((m, QKV_W), BF16)],
        scratch_shapes=[pltpu.VMEM((tm, d), BF16)],
        compiler_params=_params("parallel", "arbitrary"),
        name="in_proj",
    )(x2, g.reshape(1, d), w)


def _attn_kernel(*refs, tq, tk, n_pre, n_pre_valid, tkp, lam_init, has_pre):
    if has_pre:
        lam_ref, g_ref, q_ref, k_ref, v_ref, kp_ref, vp_ref, o_ref, m_ref, l_ref, acc_ref = refs
    else:
        lam_ref, g_ref, q_ref, k_ref, v_ref, o_ref, m_ref, l_ref, acc_ref = refs
    qi = pl.program_id(2)

    q = q_ref[0]
    lane = lax.broadcasted_iota(jnp.int32, q.shape, 1)
    zero = jnp.zeros_like(q)
    q2 = jnp.concatenate([jnp.where(lane < ATT_DH, q, zero), jnp.where(lane >= ATT_DH, q, zero)], axis=0)

    m_ref[...] = jnp.full(m_ref.shape, -jnp.inf, F32)
    l_ref[...] = jnp.zeros(l_ref.shape, F32)
    acc_ref[...] = jnp.zeros(acc_ref.shape, F32)

    def update(kt, vt, mask):
        s = lax.dot_general(q2, kt.astype(BF16), (((1,), (1,)), ((), ())), preferred_element_type=F32)
        if mask is not None:
            s = jnp.where(mask, s, -jnp.inf)
        m_old = m_ref[...]
        m_new = jnp.maximum(m_old, jnp.max(s, axis=-1, keepdims=True))
        alpha = jnp.exp2(m_old - m_new)
        p = jnp.exp2(s - m_new)
        l_ref[...] = alpha * l_ref[...] + jnp.sum(p, axis=-1, keepdims=True)
        m_ref[...] = m_new
        acc_ref[...] = alpha * acc_ref[...] + jnp.dot(p.astype(BF16), vt.astype(BF16),
                                                      preferred_element_type=F32)

    if has_pre:
        n_tiles = n_pre // tkp
        full_tiles = n_pre_valid // tkp
        if full_tiles > 0:
            def pre_body(t, c):
                off = pl.multiple_of(t * tkp, tkp)
                update(kp_ref[0, pl.ds(off, tkp), :], vp_ref[0, pl.ds(off, tkp), :], None)
                return c
            lax.fori_loop(0, full_tiles, pre_body, 0)
        if full_tiles < n_tiles:
            col = lax.broadcasted_iota(jnp.int32, (2 * tq, tkp), 1)
            off = full_tiles * tkp
            update(kp_ref[0, off:off + tkp, :], vp_ref[0, off:off + tkp, :],
                   col < (n_pre_valid - off))

    def own_body(t, c):
        off = pl.multiple_of(t * tk, tk)
        update(k_ref[0, pl.ds(off, tk), :], v_ref[0, pl.ds(off, tk), :], None)
        return c
    lax.fori_loop(0, qi * (tq // tk), own_body, 0)

    off = pl.multiple_of(qi * tq, tq)
    if tq > CHUNK:
        row = lax.broadcasted_iota(jnp.int32, (2 * tq, tq), 0)
        col = lax.broadcasted_iota(jnp.int32, (2 * tq, tq), 1)
        row = jnp.where(row >= tq, row - tq, row)
        mask = (col // CHUNK) <= (row // CHUNK)
    else:
        mask = None
    update(k_ref[0, pl.ds(off, tq), :], v_ref[0, pl.ds(off, tq), :], mask)

    lq = lam_ref[...]
    lam = (jnp.exp(jnp.sum(lq[0:1] * lq[1:2], axis=-1, keepdims=True))
           - jnp.exp(jnp.sum(lq[2:3] * lq[3:4], axis=-1, keepdims=True)) + lam_init)
    acc = acc_ref[...]
    l = l_ref[...]
    o = acc[:tq] / l[:tq] - lam * (acc[tq:] / l[tq:])
    o_ref[0] = _rms(o, g_ref[...]) * (1.0 - lam_init)


def _attention(qkv, pre_k, pre_v, n_pre_valid, lam_p, g, lam_init, tq, heads):
    b, L, _ = qkv.shape
    has_pre = pre_k is not None
    tk = tq
    kern_kw = dict(tq=tq, tk=tk, lam_init=lam_init, has_pre=has_pre, n_pre=0, n_pre_valid=0, tkp=0)
    in_specs = [
        pl.BlockSpec((4, ATT_DH), lambda bi, h, qi: (0, 0)),
        pl.BlockSpec((1, ATT_VD), lambda bi, h, qi: (0, 0)),
        pl.BlockSpec((1, tq, 128), lambda bi, h, qi: (bi, qi, h)),
        pl.BlockSpec((1, L, 128), lambda bi, h, qi: (bi, 0, heads + h)),
        pl.BlockSpec((1, L, 128), lambda bi, h, qi: (bi, 0, 2 * heads + h)),
    ]
    args = [lam_p, g.reshape(1, ATT_VD), qkv, qkv, qkv]
    if has_pre:
        lp = pre_k.shape[1]
        tkp = min(lp, 512)
        kern_kw.update(n_pre=lp, n_pre_valid=n_pre_valid, tkp=tkp)
        in_specs += [pl.BlockSpec((1, lp, 128), lambda bi, h, qi: (bi, 0, h)),
                     pl.BlockSpec((1, lp, 128), lambda bi, h, qi: (bi, 0, h))]
        args += [pre_k, pre_v]
    return pl.pallas_call(
        functools.partial(_attn_kernel, **kern_kw),
        grid=(b, heads, L // tq),
        in_specs=in_specs,
        out_specs=pl.BlockSpec((1, tq, 128), lambda bi, h, qi: (bi, qi, h)),
        out_shape=jax.ShapeDtypeStruct((b, L, heads * ATT_VD), F32),
        scratch_shapes=[pltpu.VMEM((2 * tq, 1), F32), pltpu.VMEM((2 * tq, 1), F32),
                        pltpu.VMEM((2 * tq, ATT_VD), F32)],
        compiler_params=_params("parallel", "parallel", "arbitrary"),
        name="diff_attn",
    )(*args)


def _attn_cached_kernel(lam_ref, g_ref, qkv_ref, kc_ref, vc_ref, o_ref, q2t_ref, m_ref, l_ref, acc_ref,
                        *, tq, heads, lam_init):
    t = pl.program_id(1)
    nt = (((1,), (1,)), ((), ()))
    tn = (((0,), (0,)), ((), ()))
    w = heads * 128

    @pl.when(t == 0)
    def _():
        r = lax.broadcasted_iota(jnp.int32, (128, 128), 0)
        c = lax.broadcasted_iota(jnp.int32, (128, 128), 1)
        e1 = jnp.where((r == c) & (r < ATT_DH), 1.0, 0.0).astype(BF16)
        e2 = jnp.where((r == c) & (r >= ATT_DH), 1.0, 0.0).astype(BF16)
        for h in range(heads):
            q = qkv_ref[0, :, h * 128:(h + 1) * 128]
            q2t_ref[h] = jnp.concatenate([lax.dot_general(e1, q, nt, preferred_element_type=F32),
                                          lax.dot_general(e2, q, nt, preferred_element_type=F32)],
                                         axis=1).astype(BF16)
        m_ref[...] = jnp.full(m_ref.shape, -jnp.inf, F32)
        l_ref[...] = jnp.zeros(l_ref.shape, F32)
        acc_ref[...] = jnp.zeros(acc_ref.shape, F32)

    def absorb(h, kt, vt):
        s = jnp.dot(kt.astype(BF16), q2t_ref[h], preferred_element_type=F32)
        m_old = m_ref[h]
        m_new = jnp.maximum(m_old, jnp.max(s, axis=0, keepdims=True))
        alpha = jnp.exp2(m_old - m_new)
        p = jnp.exp2(s - m_new)
        l_ref[h] = alpha * l_ref[h] + jnp.sum(p, axis=0, keepdims=True)
        m_ref[h] = m_new
        acc_ref[h] = alpha * acc_ref[h] + lax.dot_general(vt.astype(BF16), p.astype(BF16), tn,
                                                          preferred_element_type=F32)

    for h in range(heads):
        absorb(h, kc_ref[0, 0, :, h, :], vc_ref[0, 0, :, h, :])

    @pl.when(t == pl.num_programs(1) - 1)
    def _():
        lq = lam_ref[...]
        lam = (jnp.exp(jnp.sum(lq[0:1] * lq[1:2], axis=-1, keepdims=True))
               - jnp.exp(jnp.sum(lq[2:3] * lq[3:4], axis=-1, keepdims=True)) + lam_init)
        for h in range(heads):
            absorb(h, qkv_ref[0, :, w + h * 128:w + (h + 1) * 128],
                   qkv_ref[0, :, 2 * w + h * 128:2 * w + (h + 1) * 128])
            n = (acc_ref[h] / l_ref[h]).T
            o = n[:tq] - lam * n[tq:]
            o_ref[0, :, h * 128:(h + 1) * 128] = _rms(o, g_ref[...]) * (1.0 - lam_init)


def _attention_cached(qkv, cache_k, cache_v, layer, lam_p, g, lam_init, tkc):
    b, tq, _ = qkv.shape
    _, _, past, heads, vd = cache_k.shape
    cache_spec = pl.BlockSpec((1, 1, tkc, heads, vd), lambda bi, t: (layer, bi, t, 0, 0))
    kern = functools.partial(_attn_cached_kernel, tq=tq, heads=heads, lam_init=lam_init)
    return pl.pallas_call(
        kern,
        grid=(b, past // tkc),
        in_specs=[
            pl.BlockSpec((4, ATT_DH), lambda bi, t: (0, 0)),
            pl.BlockSpec((1, ATT_VD), lambda bi, t: (0, 0)),
            pl.BlockSpec((1, tq, qkv.shape[-1]), lambda bi, t: (bi, 0, 0)),
            cache_spec, cache_spec,
        ],
        out_specs=pl.BlockSpec((1, tq, heads * ATT_VD), lambda bi, t: (bi, 0, 0)),
        out_shape=jax.ShapeDtypeStruct((b, tq, heads * ATT_VD), F32),
        scratch_shapes=[pltpu.VMEM((heads, 128, 2 * tq), BF16), pltpu.VMEM((heads, 1, 2 * tq), F32),
                        pltpu.VMEM((heads, 1, 2 * tq), F32), pltpu.VMEM((heads, ATT_VD, 2 * tq), F32)],
        compiler_params=_params("parallel", "arbitrary"),
        name="diff_attn_cached",
    )(lam_p, g.reshape(1, ATT_VD), qkv, cache_k, cache_v)


def _attn_long_kernel(lam_ref, g_ref, q_ref, k_ref, vt_ref, kp_ref, vtp_ref, o_ref,
                      sa_ref, sb_ref, m_ref, acc_ref, *, tq, tk, n_pre_valid, lam_init):
    qi = pl.program_id(2)
    nt = (((1,), (1,)), ((), ()))

    q = q_ref[0]
    r = lax.broadcasted_iota(jnp.int32, (128, 128), 0)
    c = lax.broadcasted_iota(jnp.int32, (128, 128), 1)
    e1 = jnp.where((r == c) & (r < ATT_DH), 1.0, 0.0).astype(BF16)
    e2 = jnp.where((r == c) & (r >= ATT_DH), 1.0, 0.0).astype(BF16)
    q2t = jnp.concatenate([lax.dot_general(e1, q, nt, preferred_element_type=F32),
                           lax.dot_general(e2, q, nt, preferred_element_type=F32)], axis=1).astype(BF16)

    m_ref[...] = jnp.full(m_ref.shape, -jnp.inf, F32)
    acc_ref[...] = jnp.zeros(acc_ref.shape, F32)

    def scores(kt):
        return jnp.dot(kt, q2t, preferred_element_type=F32)

    def absorb(s, vt, mask):
        if mask is not None:
            s = jnp.where(mask, s, -jnp.inf)
        m_old = m_ref[...]
        m_new = jnp.maximum(m_old, jnp.max(s, axis=0, keepdims=True))
        alpha = jnp.exp2(m_old - m_new)
        p = jnp.exp2(s - m_new).astype(BF16)
        m_ref[...] = m_new
        acc_ref[...] = alpha * acc_ref[...] + jnp.dot(vt, p, preferred_element_type=F32)

    def step(cur_ref, nxt_ref, t):
        off = pl.multiple_of((t + 1) * tk, tk)
        nxt_ref[...] = scores(k_ref[0, pl.ds(off, tk), :])
        absorb(cur_ref[...], vt_ref[0, 0, t], None)

    n_pre = kp_ref.shape[1]
    key = lax.broadcasted_iota(jnp.int32, (n_pre, 2 * tq), 0)
    absorb(scores(kp_ref[0]), vtp_ref[0, 0], key < n_pre_valid)

    sa_ref[...] = scores(k_ref[0, 0:tk, :])

    def pair(u, carry):
        step(sa_ref, sb_ref, 2 * u)
        step(sb_ref, sa_ref, 2 * u + 1)
        return carry
    lax.fori_loop(0, qi, pair, 0)

    key = lax.broadcasted_iota(jnp.int32, (tk, 2 * tq), 0)
    qry = lax.broadcasted_iota(jnp.int32, (tk, 2 * tq), 1)
    qry = jnp.where(qry >= tq, qry - tq, qry)
    shift = CHUNK.bit_length() - 1
    off = pl.multiple_of((2 * qi + 1) * tk, tk)
    sb_ref[...] = scores(k_ref[0, pl.ds(off, tk), :])
    absorb(sa_ref[...], vt_ref[0, 0, 2 * qi], (key >> shift) <= (qry >> shift))
    absorb(sb_ref[...], vt_ref[0, 0, 2 * qi + 1], ((key + tk) >> shift) <= (qry >> shift))

    lq = lam_ref[...]
    lam = (jnp.exp(jnp.sum(lq[0:1] * lq[1:2], axis=-1, keepdims=True))
           - jnp.exp(jnp.sum(lq[2:3] * lq[3:4], axis=-1, keepdims=True)) + lam_init)
    acc = acc_ref[...]
    num = acc[:ATT_VD]
    den = acc[ATT_VD:ATT_VD + 1]
    o_t = num[:, :tq] / den[:, :tq] - lam * (num[:, tq:] / den[:, tq:])
    o_ref[0] = _rms(o_t.T, g_ref[...]) * (1.0 - lam_init)


VT_ROWS = ATT_VD + 16


def _values_t(v):
    ones = jnp.ones(v.shape[:-1] + (1,), v.dtype)
    zeros = jnp.zeros(v.shape[:-1] + (VT_ROWS - ATT_VD - 1,), v.dtype)
    return jnp.swapaxes(jnp.concatenate([v, ones, zeros], axis=-1), -1, -2)


def _attention_long(qkv, pre_k, pre_vt, n_pre_valid, lam_p, g, lam_init, tq, heads):
    b, L, _ = qkv.shape
    tk = tq // 2
    nt = L // tk
    lp = pre_k.shape[1]
    v = qkv[:, :, 2 * heads * 128:].reshape(b, nt, tk, heads, ATT_VD)
    vt = _values_t(v.transpose(0, 3, 1, 2, 4))
    kern = functools.partial(_attn_long_kernel, tq=tq, tk=tk, n_pre_valid=n_pre_valid, lam_init=lam_init)
    return pl.pallas_call(
        kern,
        grid=(b, heads, L // tq),
        in_specs=[
            pl.BlockSpec((4, ATT_DH), lambda bi, h, qi: (0, 0)),
            pl.BlockSpec((1, ATT_VD), lambda bi, h, qi: (0, 0)),
            pl.BlockSpec((1, tq, 128), lambda bi, h, qi: (bi, qi, h)),
            pl.BlockSpec((1, L, 128), lambda bi, h, qi: (bi, 0, heads + h)),
            pl.BlockSpec((1, 1, nt, VT_ROWS, tk), lambda bi, h, qi: (bi, h, 0, 0, 0)),
            pl.BlockSpec((1, lp, 128), lambda bi, h, qi: (bi, 0, h)),
            pl.BlockSpec((1, 1, VT_ROWS, lp), lambda bi, h, qi: (bi, h, 0, 0)),
        ],
        out_specs=pl.BlockSpec((1, tq, 128), lambda bi, h, qi: (bi, qi, h)),
        out_shape=jax.ShapeDtypeStruct((b, L, heads * ATT_VD), F32),
        scratch_shapes=[pltpu.VMEM((tk, 2 * tq), F32), pltpu.VMEM((tk, 2 * tq), F32),
                        pltpu.VMEM((1, 2 * tq), F32), pltpu.VMEM((VT_ROWS, 2 * tq), F32)],
        compiler_params=_params("parallel", "parallel", "arbitrary"),
        name="diff_attn_long",
    )(lam_p, g.reshape(1, ATT_VD), qkv, qkv, vt, pre_k, pre_vt)


def _lru_kernel(p_ref, c0_ref, h0_ref, cw_ref, cb_ref, wa_ref, ba_ref, wx_ref, bx_ref, ll_ref,
                o_ref, hout_ref, cout_ref, ext_ref, hc_ref, *, tl, width, conv):
    li = pl.program_id(1)
    pad = 8

    @pl.when(li == 0)
    def _():
        ext_ref[pad - (conv - 1):pad] = c0_ref[0]
        hc_ref[...] = h0_ref[0]

    lx = p_ref[0, :, 0:width]
    lg = p_ref[0, :, width:2 * width]
    ext_ref[pad:pad + tl] = lx
    cw = cw_ref[...]
    xc = cb_ref[...] + cw[conv - 1:conv] * lx
    for s in range(1, conv):
        xc = xc + cw[conv - 1 - s:conv - s] * ext_ref[pad - s:pad - s + tl]
    tail = ext_ref[pad + tl - (conv - 1):pad + tl]
    cout_ref[0] = tail
    ext_ref[pad - (conv - 1):pad] = tail

    xb = xc.astype(BF16)
    r = jax.nn.sigmoid(jnp.dot(xb, wa_ref[...], preferred_element_type=F32) + ba_ref[...])
    i = jax.nn.sigmoid(jnp.dot(xb, wx_ref[...], preferred_element_type=F32) + bx_ref[...])
    z = -ll_ref[...]
    softplus = jnp.maximum(z, 0.0) + jnp.log1p(jnp.exp(-jnp.abs(z)))
    log_a = -LRU_C * r * softplus
    a = jnp.exp(log_a)
    u = jnp.sqrt(-jnp.tanh(log_a) * (1.0 + a * a)) * i * xc

    row = lax.broadcasted_iota(jnp.int32, a.shape, 0)
    d = 1
    while d < tl:
        keep = row >= d
        a_prev = jnp.where(keep, pltpu.roll(a, d, 0), 1.0)
        u_prev = jnp.where(keep, pltpu.roll(u, d, 0), 0.0)
        u = u + a * u_prev
        a = a * a_prev
        d *= 2
    h = u + a * hc_ref[...]
    hc_ref[...] = h[tl - 1:tl]
    hout_ref[0] = h[tl - 1:tl]
    o_ref[0] = h * jax.nn.gelu(lg)


def _lru(proj, c0, h0, cw, cb, wa, ba, wx, bx, ll, tl):
    b, L, _ = proj.shape
    conv, width = cw.shape
    vec = lambda a: a.reshape(1, width)
    const = lambda shape: pl.BlockSpec(shape, lambda bi, li: (0,) * len(shape))
    kern = functools.partial(_lru_kernel, tl=tl, width=width, conv=conv)
    return pl.pallas_call(
        kern,
        grid=(b, L // tl),
        in_specs=[
            pl.BlockSpec((1, tl, 2 * width), lambda bi, li: (bi, li, COL_LRU)),
            pl.BlockSpec((1, conv - 1, width), lambda bi, li: (bi, 0, 0)),
            pl.BlockSpec((1, 1, width), lambda bi, li: (bi, 0, 0)),
            const((conv, width)), const((1, width)),
            const((width, width)), const((1, width)),
            const((width, width)), const((1, width)), const((1, width)),
        ],
        out_specs=[
            pl.BlockSpec((1, tl, width), lambda bi, li: (bi, li, 0)),
            pl.BlockSpec((1, 1, width), lambda bi, li: (bi, 0, 0)),
            pl.BlockSpec((1, conv - 1, width), lambda bi, li: (bi, 0, 0)),
        ],
        out_shape=[jax.ShapeDtypeStruct((b, L, width), F32),
                   jax.ShapeDtypeStruct((b, 1, width), F32),
                   jax.ShapeDtypeStruct((b, conv - 1, width), F32)],
        scratch_shapes=[pltpu.VMEM((tl + 8, width), F32), pltpu.VMEM((1, width), F32)],
        compiler_params=_params("parallel", "arbitrary"),
        name="rg_lru",
    )(proj, c0, h0.reshape(b, 1, width), cw, vec(cb), wa, vec(ba), wx, vec(bx), vec(ll))


def _gla_kernel(pa_ref, pb_ref, s0_ref, w2_ref, gb_ref, gn_ref, o_ref, sout_ref, st_ref, oi_ref,
                *, tl, chunk, kw, vw):
    li = pl.program_id(1)

    @pl.when(li == 0)
    def _():
        st_ref[...] = s0_ref[0]

    q = pa_ref[0, :, 0:kw] * (GLA_DK ** -0.5)
    k = pa_ref[0, :, kw:2 * kw]
    v = pa_ref[0, :, 2 * kw:2 * kw + vw]
    gog = pb_ref[0, :, 0:vw]
    glr = pb_ref[0, :, vw:2 * vw]

    x = jnp.dot(glr.astype(BF16), w2_ref[...], preferred_element_type=F32) + gb_ref[...]
    log_g = (jnp.minimum(x, 0.0) - jnp.log1p(jnp.exp(-jnp.abs(x)))) / GLA_TAU

    rowc = lax.broadcasted_iota(jnp.int32, (tl, kw), 0) % chunk
    bc = log_g
    d = 1
    while d < chunk:
        bc = bc + jnp.where(rowc >= d, pltpu.roll(bc, d, 0), 0.0)
        d *= 2

    hk = lax.broadcasted_iota(jnp.int32, (kw, vw), 0) // GLA_DK
    hv = lax.broadcasted_iota(jnp.int32, (kw, vw), 1) // GLA_DV
    spread = jnp.where(hk == hv, 1.0, 0.0).astype(BF16)

    sub = min(GLA_SUB, chunk)
    row_s = rowc % sub
    o_band = jnp.dot((q * k).astype(BF16), spread, preferred_element_type=F32) * v
    for dl in range(1, sub):
        valid = row_s >= dl
        rel = jnp.where(valid, bc - pltpu.roll(bc, dl, 0), -jnp.inf)
        w = q * pltpu.roll(k, dl, 0) * jnp.exp(rel)
        att = jnp.dot(w.astype(BF16), spread, preferred_element_type=F32)
        o_band = o_band + att * pltpu.roll(v, dl, 0)
    oi_ref[...] = o_band

    n_sub = chunk // sub
    nt_dims = (((1,), (1,)), ((), ()))
    hs = lax.broadcasted_iota(jnp.int32, (vw, kw), 0) // GLA_DV
    hl = lax.broadcasted_iota(jnp.int32, (vw, kw), 1) // GLA_DK
    same_head = hs == hl
    heads = kw // GLA_DK
    if n_sub > 1:
        cat = (n_sub - 1) * kw
        kr = lax.broadcasted_iota(jnp.int32, (heads * chunk, cat), 0) // chunk
        kl = (lax.broadcasted_iota(jnp.int32, (heads * chunk, cat), 1) % kw) // GLA_DK
        key_head = kr == kl
        vr = lax.broadcasted_iota(jnp.int32, (heads * chunk, vw), 0) // chunk
        vl = lax.broadcasted_iota(jnp.int32, (heads * chunk, vw), 1) // GLA_DV
        val_head = vr == vl
        ri = lax.broadcasted_iota(jnp.int32, (chunk, kw), 0)
    for c in range(tl // chunk):
        rows = slice(c * chunk, (c + 1) * chunk)
        b_c = bc[rows]
        q_c = q[rows]
        k_c = k[rows]
        v_c = v[rows].astype(BF16)
        o_c = jnp.zeros((chunk, vw), F32)
        if n_sub > 1:
            qs, ks = [], []
            for i in range(1, n_sub):
                r_i = b_c[i * sub - 1:i * sub]
                in_q = (ri >= i * sub) & (ri < (i + 1) * sub)
                in_k = ri < i * sub
                qs.append(jnp.where(in_q, q_c * jnp.exp(jnp.where(in_q, b_c - r_i, 0.0)), 0.0))
                ks.append(jnp.where(in_k, k_c * jnp.exp(jnp.where(in_k, r_i - b_c, 0.0)), 0.0))
            q_cat = jnp.concatenate(qs, axis=1).astype(BF16)
            k_cat = jnp.concatenate(ks, axis=1).astype(BF16)
            k_bd = jnp.where(key_head, jnp.concatenate([k_cat] * heads, axis=0), jnp.zeros_like(k_cat[:1]))
            att = lax.dot_general(q_cat, k_bd, nt_dims, preferred_element_type=F32)
            v_bd = jnp.where(val_head, jnp.concatenate([v_c] * heads, axis=0), jnp.zeros_like(v_c[:1]))
            o_c = jnp.dot(att.astype(BF16), v_bd, preferred_element_type=F32)
        b_last = b_c[chunk - 1:chunk]
        st = st_ref[...]
        qe = (q_c * jnp.exp(b_c)).astype(BF16)
        o_c = o_c + lax.dot_general(qe, st.astype(BF16), nt_dims, preferred_element_type=F32)
        ke = (k_c * jnp.exp(b_last - b_c)).astype(BF16)
        kv_t = lax.dot_general(v_c, ke, (((0,), (0,)), ((), ())), preferred_element_type=F32)
        st_ref[...] = jnp.exp(b_last) * st + jnp.where(same_head, kv_t, 0.0)
        oi_ref[rows, :] += o_c
    sout_ref[0] = st_ref[...]

    o = oi_ref[...]
    o2 = o * o
    lane_h = lax.broadcasted_iota(jnp.int32, o.shape, 1) // GLA_DV
    ms = jnp.zeros_like(o)
    for h in range(vw // GLA_DV):
        in_h = lane_h == h
        s_h = jnp.sum(jnp.where(in_h, o2, 0.0), axis=-1, keepdims=True) * (1.0 / GLA_DV)
        ms = jnp.where(in_h, s_h, ms)
    o_ref[0] = o * lax.rsqrt(ms + EPS) * gn_ref[...] * jax.nn.silu(gog)


def _gla(proj, s0_t, w2p, gb, gn_t, tl):
    b, L, _ = proj.shape
    vw, kw = s0_t.shape[1:]
    chunk = min(CHUNK, L)
    const = lambda shape: pl.BlockSpec(shape, lambda bi, li: (0,) * len(shape))
    kern = functools.partial(_gla_kernel, tl=tl, chunk=chunk, kw=kw, vw=vw)
    return pl.pallas_call(
        kern,
        grid=(b, L // tl),
        in_specs=[
            pl.BlockSpec((1, tl, PROJ_TN), lambda bi, li: (bi, li, COL_GLA_A)),
            pl.BlockSpec((1, tl, PROJ_TN), lambda bi, li: (bi, li, COL_GLA_B)),
            pl.BlockSpec((1, vw, kw), lambda bi, li: (bi, 0, 0)),
            const((vw, kw)), const((1, kw)), const((1, vw)),
        ],
        out_specs=[
            pl.BlockSpec((1, tl, vw), lambda bi, li: (bi, li, 0)),
            pl.BlockSpec((1, vw, kw), lambda bi, li: (bi, 0, 0)),
        ],
        out_shape=[jax.ShapeDtypeStruct((b, L, vw), F32), jax.ShapeDtypeStruct((b, vw, kw), F32)],
        scratch_shapes=[pltpu.VMEM((vw, kw), F32), pltpu.VMEM((tl, vw), F32)],
        compiler_params=_params("parallel", "arbitrary"),
        name="gla",
    )(proj, proj, s0_t, w2p, gb.reshape(1, kw), gn_t.reshape(1, vw))


def _ffn_kernel(x_ref, oa_ref, ol_ref, og_ref, wo_ref, g_ref, c0_ref, wu_ref, wg_ref, cw_ref, cb_ref,
                wd_ref, gf_ref, out_ref, cout_ref, acc_ref, hn_ref, carry_ref,
                *, bb, tl, tn, conv, nsplit, final_norm):
    li = pl.program_id(1)
    j = pl.program_id(2)
    m = bb * tl
    d = x_ref.shape[-1]
    wa = oa_ref.shape[-1]
    wl = ol_ref.shape[-1]

    @pl.when(j == 0)
    def _():
        mix = jnp.dot(oa_ref[...].reshape(m, wa).astype(BF16), wo_ref[0:wa], preferred_element_type=F32)
        mix += jnp.dot(ol_ref[...].reshape(m, wl).astype(BF16), wo_ref[wa:wa + wl],
                       preferred_element_type=F32)
        mix += jnp.dot(og_ref[...].reshape(m, d - wa - wl).astype(BF16), wo_ref[wa + wl:d],
                       preferred_element_type=F32)
        xn = x_ref[...].reshape(m, d) + mix
        acc_ref[...] = xn
        hn_ref[...] = _rms(xn, g_ref[...]).astype(BF16)

    @pl.when(li == 0)
    def _():
        carry_ref[j] = c0_ref[...]

    prev = carry_ref[j]
    ts = tl // nsplit
    t = lax.broadcasted_iota(jnp.int32, (bb, ts, tn), 1)
    cw = cw_ref[...]
    def up_gate(p):
        hn = hn_ref[p * ts * bb:(p + 1) * ts * bb]
        return (jnp.dot(hn, wu_ref[...], preferred_element_type=F32),
                jnp.dot(hn, wg_ref[...], preferred_element_type=F32))

    ahead = up_gate(0)
    for p in range(nsplit):
        rows = slice(p * ts * bb, (p + 1) * ts * bb)
        u, gate = ahead
        if p + 1 < nsplit:
            ahead = up_gate(p + 1)
        u3 = u.reshape(bb, ts, tn)
        uc = cb_ref[...] + cw[conv - 1:conv] * u3
        for s in range(1, conv):
            shifted = pltpu.roll(u, s, 0).reshape(bb, ts, tn)
            for r in range(s):
                shifted = jnp.where(t == r, prev[:, conv - 1 - s + r:conv - s + r, :], shifted)
            uc = uc + cw[conv - 1 - s:conv - s] * shifted
        prev = u3[:, ts - (conv - 1):ts, :]
        act = (jax.nn.gelu(uc) * gate.reshape(bb, ts, tn)).reshape(bb * ts, tn).astype(BF16)
        acc_ref[rows] += jnp.dot(act, wd_ref[...], preferred_element_type=F32)
    carry_ref[j] = prev
    cout_ref[:, j] = prev

    @pl.when(j == pl.num_programs(2) - 1)
    def _():
        y = acc_ref[...]
        if final_norm:
            y = _rms(y, gf_ref[...])
        out_ref[...] = y.reshape(bb, tl, d)


def _ffn(x, oa, ol, og, wo, g, c0, wu, wg, cw, cb, wd, gf, bb, tl, tn, final_norm):
    b, L, d = x.shape
    conv, f = cw.shape
    nj = f // tn
    nsplit = tl // FFN_PIECE if bb == 1 and tl % FFN_PIECE == 0 else 1
    kern = functools.partial(_ffn_kernel, bb=bb, tl=tl, tn=tn, conv=conv, nsplit=nsplit,
                             final_norm=final_norm)
    act_spec = lambda w: pl.BlockSpec((bb, tl, w), lambda bi, li, j: (bi, li, 0))
    const = lambda shape: pl.BlockSpec(shape, lambda bi, li, j: (0,) * len(shape))
    return pl.pallas_call(
        kern,
        grid=(b // bb, L // tl, nj),
        in_specs=[
            act_spec(d), act_spec(oa.shape[-1]), act_spec(ol.shape[-1]), act_spec(og.shape[-1]),
            const((d, d)), const((1, d)),
            pl.BlockSpec((bb, conv - 1, tn), lambda bi, li, j: (bi, 0, j)),
            pl.BlockSpec((d, tn), lambda bi, li, j: (0, j)),
            pl.BlockSpec((d, tn), lambda bi, li, j: (0, j)),
            pl.BlockSpec((conv, tn), lambda bi, li, j: (0, j)),
            pl.BlockSpec((1, tn), lambda bi, li, j: (0, j)),
            pl.BlockSpec((tn, d), lambda bi, li, j: (j, 0)),
            const((1, d)),
        ],
        out_specs=[
            pl.BlockSpec((bb, tl, d), lambda bi, li, j: (bi, li, 0)),
            pl.BlockSpec((bb, nj, conv - 1, tn), lambda bi, li, j: (bi, 0, 0, 0)),
        ],
        out_shape=[jax.ShapeDtypeStruct((b, L, d), F32), jax.ShapeDtypeStruct((b, nj, conv - 1, tn), F32)],
        scratch_shapes=[pltpu.VMEM((bb * tl, d), F32), pltpu.VMEM((bb * tl, d), BF16),
                        pltpu.VMEM((nj, bb, conv - 1, tn), F32)],
        compiler_params=_params("parallel", "arbitrary", "arbitrary"),
        name="mix_ffn",
    )(x, oa, ol, og, wo, g.reshape(1, d), c0, wu, wg, cw, cb.reshape(1, f), wd, gf.reshape(1, d))


def _block_diag(w):
    n, c, dd = w.shape
    eye = jnp.eye(n, dtype=w.dtype)
    return jnp.einsum('ncd,nm->ncmd', w, eye).reshape(n * c, n * dd)


def _prep_layer(l, w_in, lru_gate_a_w, lru_gate_x_w, gla_gate_w2, gla_norm_g, w_out, ffn_w_up,
                ffn_w_gate, ffn_w_down, sizes):
    att_w, lru_w, gla_kw, gla_vw = sizes
    o = 0
    parts = {}
    for name, wdt in (("q", att_w), ("k", att_w), ("v", att_w), ("lx", lru_w), ("lg", lru_w),
                      ("gq", gla_kw), ("gk", gla_kw), ("gv", gla_vw), ("glr", GLA_RANK), ("gog", gla_vw)):
        parts[name] = w_in[l][:, o:o + wdt]
        o += wdt
    d = w_in.shape[1]
    pad = jnp.zeros((d, PROJ_W - o), w_in.dtype)
    order = ("q", "k", "v", "lx", "lg", "gq", "gk", "gv", "gog", "glr")
    w_in_p = jnp.concatenate([parts[n] for n in order] + [pad], axis=1).astype(BF16)
    w2p = jnp.zeros((gla_vw, gla_kw), F32).at[:GLA_RANK].set(gla_gate_w2[l]).astype(BF16)
    return dict(
        w_in=w_in_p,
        wa=_block_diag(lru_gate_a_w[l]).astype(BF16),
        wx=_block_diag(lru_gate_x_w[l]).astype(BF16),
        w2p=w2p,
        gn_t=jnp.tile(gla_norm_g[l], gla_vw // GLA_DV),
        w_out=w_out[l].astype(BF16),
        wu=ffn_w_up[l].astype(BF16), wg=ffn_w_gate[l].astype(BF16), wd=ffn_w_down[l].astype(BF16),
    )


def _state_to_bd(s):
    b, h, dk, dv = s.shape
    eye = jnp.eye(h, dtype=s.dtype)
    return jnp.einsum('bhde,hg->bhegd', s, eye).reshape(b, h * dv, h * dk)


def _state_from_bd(st, h):
    b, vw, kw = st.shape
    dv, dk = vw // h, kw // h
    s5 = st.reshape(b, h, dv, h, dk)
    diag = jnp.stack([s5[:, i, :, i, :] for i in range(h)], axis=1)
    return diag.transpose(0, 1, 3, 2)


def _pick_tile(n, target):
    t = min(n, target)
    while n % t:
        t //= 2
    return t


def _trunk(x, states, pre_kv, n_pre_valid, layers, prm, heads, gla_heads, tiles, long_attn=False,
           cache=None):
    b, L, d = x.shape
    depth = len(layers)
    tm, tq, tl_lru, tl_gla, bb, tl_ffn, tn = tiles
    outs = []
    for l in range(depth):
        lw = layers[l]
        width = prm['lru_conv_w'].shape[-1]
        f = prm['ffn_conv_w'].shape[-1]
        if states is None:
            h0 = jnp.zeros((b, width), F32)
            c0 = jnp.zeros((b, prm['lru_conv_w'].shape[1] - 1, width), F32)
            s0 = jnp.zeros((b, gla_heads, GLA_DK, GLA_DV), F32)
            fc0 = jnp.zeros((b, prm['ffn_conv_w'].shape[1] - 1, f), F32)
        else:
            h0, c0, s0, fc0 = states[l]
        k32, v32, pf, pb = _in_proj(x.reshape(b * L, d), prm['norm_mix_g'][l], lw['w_in'], tm)
        pf = pf.reshape(b, L, MIX_W)
        pb = pb.reshape(b, L, QKV_W)
        lam_init = 0.8 - 0.6 * math.exp(-0.3 * l)
        lam_p, sub_g = prm['attn_lambda'][l], prm['attn_subln_g'][l]
        if cache is not None:
            o_att = _attention_cached(pb, cache[0], cache[1], l, lam_p, sub_g, lam_init,
                                      _pick_tile(cache[0].shape[2], 1024))
        elif long_attn:
            o_att = _attention_long(pb, pre_kv[l][0], pre_kv[l][1], n_pre_valid, lam_p, sub_g, lam_init,
                                    tq, heads)
        else:
            o_att = _attention(pb, None, None, 0, lam_p, sub_g, lam_init, tq, heads)
        o_lru, h1, c1 = _lru(pf, c0, h0, prm['lru_conv_w'][l], prm['lru_conv_b'][l], lw['wa'],
                             prm['lru_gate_a_b'][l], lw['wx'], prm['lru_gate_x_b'][l],
                             prm['lru_log_lambda'][l], tl_lru)
        o_gla, s1_t = _gla(pf, _state_to_bd(s0), lw['w2p'], prm['gla_gate_b'][l], lw['gn_t'], tl_gla)
        x, fc1 = _ffn(x, o_att, o_lru, o_gla, lw['w_out'], prm['norm_ffn_g'][l], fc0, lw['wu'], lw['wg'],
                      prm['ffn_conv_w'][l], prm['ffn_conv_b'][l], lw['wd'], prm['norm_final_g'],
                      bb, tl_ffn, tn, final_norm=(l == depth - 1))
        fc1 = fc1.transpose(0, 2, 1, 3).reshape(b, fc1.shape[2], f)
        aw = heads * ATT_VD
        outs.append(dict(k=k32.reshape(b, L, aw), v=v32.reshape(b, L, aw), kb=pb[:, :, aw:2 * aw],
                         vb=pb[:, :, 2 * aw:3 * aw], h=h1.reshape(b, width), c=c1,
                         s=_state_from_bd(s1_t, gla_heads), fc=fc1))
    return x, outs


def kernel(x_prompt, x_sample, cache_attn_k, cache_attn_v, state_lru_h, state_lru_conv, state_gla, state_ffn_conv, meta_tokens, norm_mix_g, w_in, attn_lambda, attn_subln_g, lru_conv_w, lru_conv_b, lru_gate_a_w, lru_gate_a_b, lru_gate_x_w, lru_gate_x_b, lru_log_lambda, gla_gate_w2, gla_gate_b, gla_norm_g, w_out, norm_ffn_g, ffn_w_up, ffn_conv_w, ffn_conv_b, ffn_w_gate, ffn_w_down, norm_final_g):
    depth = w_in.shape[0]
    b, seq, d = x_prompt.shape
    db, dseq, _ = x_sample.shape
    past = cache_attn_k.shape[2]
    heads = cache_attn_k.shape[3]
    gla_heads = state_gla.shape[2]
    n_meta = meta_tokens.shape[0]
    width = lru_conv_w.shape[-1]
    f = ffn_conv_w.shape[-1]
    att_w = heads * ATT_VD
    sizes = (att_w, width, gla_heads * GLA_DK, gla_heads * GLA_DV)

    prm = dict(norm_mix_g=norm_mix_g, attn_lambda=attn_lambda, attn_subln_g=attn_subln_g,
               lru_conv_w=lru_conv_w, lru_conv_b=lru_conv_b, lru_gate_a_b=lru_gate_a_b,
               lru_gate_x_b=lru_gate_x_b, lru_log_lambda=lru_log_lambda, gla_gate_b=gla_gate_b,
               norm_ffn_g=norm_ffn_g, ffn_conv_w=ffn_conv_w, ffn_conv_b=ffn_conv_b,
               norm_final_g=norm_final_g)
    layers = [_prep_layer(l, w_in, lru_gate_a_w, lru_gate_x_w, gla_gate_w2, gla_norm_g, w_out,
                          ffn_w_up, ffn_w_gate, ffn_w_down, sizes) for l in range(depth)]
    tn = 256 if f % 256 == 0 else 128

    xm = jnp.broadcast_to(meta_tokens.astype(F32)[None], (b, n_meta, d))
    tiles_m = (b * n_meta, n_meta, n_meta, n_meta, b, n_meta, tn)
    _, om = _trunk(xm, None, None, 0, layers, prm, heads, gla_heads, tiles_m)

    pad_rows = 128 - n_meta
    pre_f = [(jnp.pad(o['kb'], ((0, 0), (0, pad_rows), (0, 0))),
              _values_t(jnp.pad(o['vb'], ((0, 0), (0, pad_rows), (0, 0)))
                        .reshape(b, 128, heads, ATT_VD).transpose(0, 2, 1, 3))) for o in om]
    st_f = [(o['h'], o['c'], o['s'], o['fc']) for o in om]
    tiles_f = (_pick_tile(b * seq, 1024), _pick_tile(seq, 1024), _pick_tile(seq, 512),
               _pick_tile(seq, 256), 1, _pick_tile(seq, 1024), tn)
    yp, of = _trunk(x_prompt, st_f, pre_f, n_meta, layers, prm, heads, gla_heads, tiles_f,
                    long_attn=True)

    st_s = [(state_lru_h[l], state_lru_conv[l], state_gla[l], state_ffn_conv[l]) for l in range(depth)]
    tiles_s = (_pick_tile(db * dseq, 1024), dseq, dseq, dseq, db, dseq, tn)
    ys, os_ = _trunk(x_sample, st_s, None, past, layers, prm, heads, gla_heads, tiles_s,
                     cache=(cache_attn_k, cache_attn_v))

    def stack_p(name, tail_shape):
        return jnp.stack([jnp.concatenate([m[name], fr[name]], axis=1).reshape((b, n_meta + seq) + tail_shape)
                          for m, fr in zip(om, of)])

    k_p = stack_p('k', (heads, ATT_VD))
    v_p = stack_p('v', (heads, ATT_VD))
    k_s = jnp.stack([o['k'].reshape(db, dseq, heads, ATT_VD) for o in os_])
    v_s = jnp.stack([o['v'].reshape(db, dseq, heads, ATT_VD) for o in os_])
    st = lambda outs, name: jnp.stack([o[name] for o in outs])
    return (yp, ys, k_p, v_p, st(of, 'h'), st(of, 'c'), st(of, 's'), st(of, 'fc'),
            k_s, v_s, st(os_, 'h'), st(os_, 'c'), st(os_, 's'), st(os_, 'fc'))
```

```python
import functools
import math

import jax
import jax.numpy as jnp
from jax import lax
from jax.experimental import pallas as pl
from jax.experimental.pallas import tpu as pltpu

F32 = jnp.float32
BF16 = jnp.bfloat16

CHUNK = 64
EPS = 1e-6
ATT_DH = 64
ATT_VD = 128
LRU_C = 8.0
GLA_DK = 32
GLA_DV = 64
GLA_RANK = 16
GLA_TAU = 16.0
GLA_SUB = 16
FFN_PIECE = 256

PROJ_TN = 512
PROJ_W = 3072
QKV_W = 1536
MIX_W = PROJ_W - QKV_W
COL_LRU = 0
COL_GLA_A = 1
COL_GLA_B = 2

VMEM_LIMIT = 48 * 1024 * 1024


def _rms(x, g):
    return x * lax.rsqrt(jnp.mean(x * x, axis=-1, keepdims=True) + EPS) * g


def _params(*sem):
    return pltpu.CompilerParams(dimension_semantics=sem, vmem_limit_bytes=VMEM_LIMIT)


def _in_proj_kernel(x_ref, g_ref, w_ref, k_ref, v_ref, mix_ref, ob_ref, hn_ref, *, n_bf_tiles, q_scale):
    j = pl.program_id(1)

    @pl.when(j == 0)
    def _():
        hn_ref[...] = _rms(x_ref[...], g_ref[...]).astype(BF16)

    y = jnp.dot(hn_ref[...], w_ref[...], preferred_element_type=F32)

    @pl.when(j == 1)
    def _():
        for h in range(k_ref.shape[1]):
            k_ref[:, h, :] = y[:, h * 128:(h + 1) * 128]

    @pl.when(j == 2)
    def _():
        for h in range(v_ref.shape[1]):
            v_ref[:, h, :] = y[:, h * 128:(h + 1) * 128]

    @pl.when(j >= n_bf_tiles)
    def _():
        mix_ref[...] = y

    @pl.when(j < n_bf_tiles)
    def _():
        scale = jnp.where(j == 0, q_scale, 1.0).astype(F32)
        ob_ref[...] = (y * scale).astype(BF16)


def _in_proj(x2, g, w, tm):
    m, d = x2.shape
    n_bf = QKV_W // PROJ_TN
    kv_heads = PROJ_TN // 128
    kern = functools.partial(_in_proj_kernel, n_bf_tiles=n_bf, q_scale=ATT_DH ** -0.5 * math.log2(math.e))
    return pl.pallas_call(
        kern,
        grid=(m // tm, PROJ_W // PROJ_TN),
        in_specs=[
            pl.BlockSpec((tm, d), lambda i, j: (i, 0)),
            pl.BlockSpec((1, d), lambda i, j: (0, 0)),
            pl.BlockSpec((d, PROJ_TN), lambda i, j: (0, j)),
        ],
        out_specs=[
            pl.BlockSpec((tm, kv_heads, 128), lambda i, j: (i, 0, 0)),
            pl.BlockSpec((tm, kv_heads, 128), lambda i, j: (i, 0, 0)),
            pl.BlockSpec((tm, PROJ_TN), lambda i, j: (i, jnp.maximum(j - n_bf, 0))),
            pl.BlockSpec((tm, PROJ_TN), lambda i, j: (i, jnp.minimum(j, n_bf - 1))),
        ],
        out_shape=[jax.ShapeDtypeStruct((m, kv_heads, 128), F32), jax.ShapeDtypeStruct((m, kv_heads, 128), F32),
                   jax.ShapeDtypeStruct((m, MIX_W), F32), jax.ShapeDtypeStruct((m, QKV_W), BF16)],
        scratch_shapes=[pltpu.VMEM((tm, d), BF16)],
        compiler_params=_params("parallel", "arbitrary"),
        name="in_proj",
    )(x2, g.reshape(1, d), w)


def _attn_kernel(*refs, tq, tk, n_pre, n_pre_valid, tkp, lam_init, has_pre):
    if has_pre:
        lam_ref, g_ref, q_ref, k_ref, v_ref, kp_ref, vp_ref, o_ref, m_ref, l_ref, acc_ref = refs
    else:
        lam_ref, g_ref, q_ref, k_ref, v_ref, o_ref, m_ref, l_ref, acc_ref = refs
    qi = pl.program_id(2)

    q = q_ref[0]
    lane = lax.broadcasted_iota(jnp.int32, q.shape, 1)
    zero = jnp.zeros_like(q)
    q2 = jnp.concatenate([jnp.where(lane < ATT_DH, q, zero), jnp.where(lane >= ATT_DH, q, zero)], axis=0)

    m_ref[...] = jnp.full(m_ref.shape, -jnp.inf, F32)
    l_ref[...] = jnp.zeros(l_ref.shape, F32)
    acc_ref[...] = jnp.zeros(acc_ref.shape, F32)

    def update(kt, vt, mask):
        s = lax.dot_general(q2, kt.astype(BF16), (((1,), (1,)), ((), ())), preferred_element_type=F32)
        if mask is not None:
            s = jnp.where(mask, s, -jnp.inf)
        m_old = m_ref[...]
        m_new = jnp.maximum(m_old, jnp.max(s, axis=-1, keepdims=True))
        alpha = jnp.exp2(m_old - m_new)
        p = jnp.exp2(s - m_new)
        l_ref[...] = alpha * l_ref[...] + jnp.sum(p, axis=-1, keepdims=True)
        m_ref[...] = m_new
        acc_ref[...] = alpha * acc_ref[...] + jnp.dot(p.astype(BF16), vt.astype(BF16),
                                                      preferred_element_type=F32)

    if has_pre:
        n_tiles = n_pre // tkp
        full_tiles = n_pre_valid // tkp
        if full_tiles > 0:
            def pre_body(t, c):
                off = pl.multiple_of(t * tkp, tkp)
                update(kp_ref[0, pl.ds(off, tkp), :], vp_ref[0, pl.ds(off, tkp), :], None)
                return c
            lax.fori_loop(0, full_tiles, pre_body, 0)
        if full_tiles < n_tiles:
            col = lax.broadcasted_iota(jnp.int32, (2 * tq, tkp), 1)
            off = full_tiles * tkp
            update(kp_ref[0, off:off + tkp, :], vp_ref[0, off:off + tkp, :],
                   col < (n_pre_valid - off))

    def own_body(t, c):
        off = pl.multiple_of(t * tk, tk)
        update(k_ref[0, pl.ds(off, tk), :], v_ref[0, pl.ds(off, tk), :], None)
        return c
    lax.fori_loop(0, qi * (tq // tk), own_body, 0)

    off = pl.multiple_of(qi * tq, tq)
    if tq > CHUNK:
        row = lax.broadcasted_iota(jnp.int32, (2 * tq, tq), 0)
        col = lax.broadcasted_iota(jnp.int32, (2 * tq, tq), 1)
        row = jnp.where(row >= tq, row - tq, row)
        mask = (col // CHUNK) <= (row // CHUNK)
    else:
        mask = None
    update(k_ref[0, pl.ds(off, tq), :], v_ref[0, pl.ds(off, tq), :], mask)

    lq = lam_ref[...]
    lam = (jnp.exp(jnp.sum(lq[0:1] * lq[1:2], axis=-1, keepdims=True))
           - jnp.exp(jnp.sum(lq[2:3] * lq[3:4], axis=-1, keepdims=True)) + lam_init)
    acc = acc_ref[...]
    l = l_ref[...]
    o = acc[:tq] / l[:tq] - lam * (acc[tq:] / l[tq:])
    o_ref[0] = _rms(o, g_ref[...]) * (1.0 - lam_init)


def _attention(qkv, pre_k, pre_v, n_pre_valid, lam_p, g, lam_init, tq, heads):
    b, L, _ = qkv.shape
    has_pre = pre_k is not None
    tk = tq
    kern_kw = dict(tq=tq, tk=tk, lam_init=lam_init, has_pre=has_pre, n_pre=0, n_pre_valid=0, tkp=0)
    in_specs = [
        pl.BlockSpec((4, ATT_DH), lambda bi, h, qi: (0, 0)),
        pl.BlockSpec((1, ATT_VD), lambda bi, h, qi: (0, 0)),
        pl.BlockSpec((1, tq, 128), lambda bi, h, qi: (bi, qi, h)),
        pl.BlockSpec((1, L, 128), lambda bi, h, qi: (bi, 0, heads + h)),
        pl.BlockSpec((1, L, 128), lambda bi, h, qi: (bi, 0, 2 * heads + h)),
    ]
    args = [lam_p, g.reshape(1, ATT_VD), qkv, qkv, qkv]
    if has_pre:
        lp = pre_k.shape[1]
        tkp = min(lp, 512)
        kern_kw.update(n_pre=lp, n_pre_valid=n_pre_valid, tkp=tkp)
        in_specs += [pl.BlockSpec((1, lp, 128), lambda bi, h, qi: (bi, 0, h)),
                     pl.BlockSpec((1, lp, 128), lambda bi, h, qi: (bi, 0, h))]
        args += [pre_k, pre_v]
    return pl.pallas_call(
        functools.partial(_attn_kernel, **kern_kw),
        grid=(b, heads, L // tq),
        in_specs=in_specs,
        out_specs=pl.BlockSpec((1, tq, 128), lambda bi, h, qi: (bi, qi, h)),
        out_shape=jax.ShapeDtypeStruct((b, L, heads * ATT_VD), F32),
        scratch_shapes=[pltpu.VMEM((2 * tq, 1), F32), pltpu.VMEM((2 * tq, 1), F32),
                        pltpu.VMEM((2 * tq, ATT_VD), F32)],
        compiler_params=_params("parallel", "parallel", "arbitrary"),
        name="diff_attn",
    )(*args)


def _attn_cached_kernel(lam_ref, g_ref, qkv_ref, kc_ref, vc_ref, o_ref,
                        q2t_ref, m_ref, l_ref, acc_ref, *, tq, heads, lam_init):
    t = pl.program_id(1)
    nt = (((1,), (1,)), ((), ()))
    tn = (((0,), (0,)), ((), ()))
    w = heads * 128
    lanes = heads * 2 * tq

    @pl.when(t == 0)
    def _():
        r = lax.broadcasted_iota(jnp.int32, (128, 128), 0)
        c = lax.broadcasted_iota(jnp.int32, (128, 128), 1)
        e1 = jnp.where((r == c) & (r < ATT_DH), 1.0, 0.0).astype(BF16)
        e2 = jnp.where((r == c) & (r >= ATT_DH), 1.0, 0.0).astype(BF16)
        parts = []
        for h in range(heads):
            q = qkv_ref[0, :, h * 128:(h + 1) * 128]
            parts += [lax.dot_general(e1, q, nt, preferred_element_type=F32),
                      lax.dot_general(e2, q, nt, preferred_element_type=F32)]
        q2t_ref[...] = jnp.concatenate(parts, axis=1).astype(BF16)
        m_ref[...] = jnp.full(m_ref.shape, -jnp.inf, F32)
        l_ref[...] = jnp.zeros(l_ref.shape, F32)
        acc_ref[...] = jnp.zeros(acc_ref.shape, F32)

    def absorb(kt, vt, row_head):
        s = jnp.dot(kt.astype(BF16), q2t_ref[...], preferred_element_type=F32)
        lane_head = lax.broadcasted_iota(jnp.int32, s.shape, 1) // (2 * tq)
        s = jnp.where(row_head == lane_head, s, -jnp.inf)
        m_old = m_ref[...]
        m_new = jnp.maximum(m_old, jnp.max(s, axis=0, keepdims=True))
        alpha = jnp.exp2(m_old - m_new)
        p = jnp.exp2(s - m_new)
        l_ref[...] = alpha * l_ref[...] + jnp.sum(p, axis=0, keepdims=True)
        m_ref[...] = m_new
        acc_ref[...] = alpha * acc_ref[...] + lax.dot_general(vt.astype(BF16), p.astype(BF16), tn,
                                                              preferred_element_type=F32)

    rows_c = kc_ref.shape[0]
    absorb(kc_ref[...], vc_ref[...], lax.broadcasted_iota(jnp.int32, (rows_c, lanes), 0) % heads)

    @pl.when(t == pl.num_programs(1) - 1)
    def _():
        lq = lam_ref[...]
        lam = (jnp.exp(jnp.sum(lq[0:1] * lq[1:2], axis=-1, keepdims=True))
               - jnp.exp(jnp.sum(lq[2:3] * lq[3:4], axis=-1, keepdims=True)) + lam_init)
        k_own = jnp.concatenate([qkv_ref[0, :, w + h * 128:w + (h + 1) * 128] for h in range(heads)], axis=0)
        v_own = jnp.concatenate([qkv_ref[0, :, 2 * w + h * 128:2 * w + (h + 1) * 128] for h in range(heads)],
                                axis=0)
        absorb(k_own, v_own, lax.broadcasted_iota(jnp.int32, (heads * tq, lanes), 0) // tq)
        n = (acc_ref[...] / l_ref[...]).T
        for h in range(heads):
            o = n[2 * h * tq:(2 * h + 1) * tq] - lam * n[(2 * h + 1) * tq:(2 * h + 2) * tq]
            o_ref[0, :, h * 128:(h + 1) * 128] = _rms(o, g_ref[...]) * (1.0 - lam_init)


def _attention_cached(qkv, cache_k, cache_v, layer, lam_p, g, lam_init, tkc):
    b, tq, _ = qkv.shape
    _, _, past, heads, vd = cache_k.shape
    depth = cache_k.shape[0]
    lanes = heads * 2 * tq
    flat = lambda c: c.reshape(depth, b, past * heads, vd)
    cache_spec = pl.BlockSpec((None, None, tkc * heads, vd), lambda bi, t: (layer, bi, t, 0))
    kern = functools.partial(_attn_cached_kernel, tq=tq, heads=heads, lam_init=lam_init)
    return pl.pallas_call(
        kern,
        grid=(b, past // tkc),
        in_specs=[
            pl.BlockSpec((4, ATT_DH), lambda bi, t: (0, 0)),
            pl.BlockSpec((1, ATT_VD), lambda bi, t: (0, 0)),
            pl.BlockSpec((1, tq, qkv.shape[-1]), lambda bi, t: (bi, 0, 0)),
            cache_spec, cache_spec,
        ],
        out_specs=pl.BlockSpec((1, tq, heads * ATT_VD), lambda bi, t: (bi, 0, 0)),
        out_shape=jax.ShapeDtypeStruct((b, tq, heads * ATT_VD), F32),
        scratch_shapes=[pltpu.VMEM((128, lanes), BF16), pltpu.VMEM((1, lanes), F32),
                        pltpu.VMEM((1, lanes), F32), pltpu.VMEM((ATT_VD, lanes), F32)],
        compiler_params=_params("parallel", "arbitrary"),
        name="diff_attn_cached",
    )(lam_p, g.reshape(1, ATT_VD), qkv, flat(cache_k), flat(cache_v))


def _attn_long_kernel(lam_ref, g_ref, q_ref, k_ref, vt_ref, kp_ref, vtp_ref, o_ref,
                      sa_ref, sb_ref, m_ref, acc_ref, *, tq, tk, n_pre_valid, lam_init):
    qi = pl.program_id(2)
    nt = (((1,), (1,)), ((), ()))

    q = q_ref[0]
    r = lax.broadcasted_iota(jnp.int32, (128, 128), 0)
    c = lax.broadcasted_iota(jnp.int32, (128, 128), 1)
    e1 = jnp.where((r == c) & (r < ATT_DH), 1.0, 0.0).astype(BF16)
    e2 = jnp.where((r == c) & (r >= ATT_DH), 1.0, 0.0).astype(BF16)
    q2t = jnp.concatenate([lax.dot_general(e1, q, nt, preferred_element_type=F32),
                           lax.dot_general(e2, q, nt, preferred_element_type=F32)], axis=1).astype(BF16)

    m_ref[...] = jnp.full(m_ref.shape, -jnp.inf, F32)
    acc_ref[...] = jnp.zeros(acc_ref.shape, F32)

    def scores(kt):
        return jnp.dot(kt, q2t, preferred_element_type=F32)

    def absorb(s, vt, mask):
        if mask is not None:
            s = jnp.where(mask, s, -jnp.inf)
        m_old = m_ref[...]
        m_new = jnp.maximum(m_old, jnp.max(s, axis=0, keepdims=True))
        alpha = jnp.exp2(m_old - m_new)
        p = jnp.exp2(s - m_new).astype(BF16)
        m_ref[...] = m_new
        acc_ref[...] = alpha * acc_ref[...] + jnp.dot(vt, p, preferred_element_type=F32)

    def step(cur_ref, nxt_ref, t):
        off = pl.multiple_of((t + 1) * tk, tk)
        nxt_ref[...] = scores(k_ref[0, pl.ds(off, tk), :])
        absorb(cur_ref[...], vt_ref[0, 0, t], None)

    n_pre = kp_ref.shape[1]
    key = lax.broadcasted_iota(jnp.int32, (n_pre, 2 * tq), 0)
    absorb(scores(kp_ref[0]), vtp_ref[0, 0], key < n_pre_valid)

    sa_ref[...] = scores(k_ref[0, 0:tk, :])

    def pair(u, carry):
        step(sa_ref, sb_ref, 2 * u)
        step(sb_ref, sa_ref, 2 * u + 1)
        return carry
    lax.fori_loop(0, qi, pair, 0)

    key = lax.broadcasted_iota(jnp.int32, (tk, 2 * tq), 0)
    qry = lax.broadcasted_iota(jnp.int32, (tk, 2 * tq), 1)
    qry = jnp.where(qry >= tq, qry - tq, qry)
    shift = CHUNK.bit_length() - 1
    off = pl.multiple_of((2 * qi + 1) * tk, tk)
    sb_ref[...] = scores(k_ref[0, pl.ds(off, tk), :])
    absorb(sa_ref[...], vt_ref[0, 0, 2 * qi], (key >> shift) <= (qry >> shift))
    absorb(sb_ref[...], vt_ref[0, 0, 2 * qi + 1], ((key + tk) >> shift) <= (qry >> shift))

    lq = lam_ref[...]
    lam = (jnp.exp(jnp.sum(lq[0:1] * lq[1:2], axis=-1, keepdims=True))
           - jnp.exp(jnp.sum(lq[2:3] * lq[3:4], axis=-1, keepdims=True)) + lam_init)
    acc = acc_ref[...]
    num = acc[:ATT_VD]
    den = acc[ATT_VD:ATT_VD + 1]
    o_t = num[:, :tq] / den[:, :tq] - lam * (num[:, tq:] / den[:, tq:])
    o_ref[0] = _rms(o_t.T, g_ref[...]) * (1.0 - lam_init)


VT_ROWS = ATT_VD + 16


def _values_t(v):
    ones = jnp.ones(v.shape[:-1] + (1,), v.dtype)
    zeros = jnp.zeros(v.shape[:-1] + (VT_ROWS - ATT_VD - 1,), v.dtype)
    return jnp.swapaxes(jnp.concatenate([v, ones, zeros], axis=-1), -1, -2)


def _attention_long(qkv, pre_k, pre_vt, n_pre_valid, lam_p, g, lam_init, tq, heads):
    b, L, _ = qkv.shape
    tk = tq // 2
    nt = L // tk
    lp = pre_k.shape[1]
    v = qkv[:, :, 2 * heads * 128:].reshape(b, nt, tk, heads, ATT_VD)
    vt = _values_t(v.transpose(0, 3, 1, 2, 4))
    kern = functools.partial(_attn_long_kernel, tq=tq, tk=tk, n_pre_valid=n_pre_valid, lam_init=lam_init)
    return pl.pallas_call(
        kern,
        grid=(b, heads, L // tq),
        in_specs=[
            pl.BlockSpec((4, ATT_DH), lambda bi, h, qi: (0, 0)),
            pl.BlockSpec((1, ATT_VD), lambda bi, h, qi: (0, 0)),
            pl.BlockSpec((1, tq, 128), lambda bi, h, qi: (bi, qi, h)),
            pl.BlockSpec((1, L, 128), lambda bi, h, qi: (bi, 0, heads + h)),
            pl.BlockSpec((1, 1, nt, VT_ROWS, tk), lambda bi, h, qi: (bi, h, 0, 0, 0)),
            pl.BlockSpec((1, lp, 128), lambda bi, h, qi: (bi, 0, h)),
            pl.BlockSpec((1, 1, VT_ROWS, lp), lambda bi, h, qi: (bi, h, 0, 0)),
        ],
        out_specs=pl.BlockSpec((1, tq, 128), lambda bi, h, qi: (bi, qi, h)),
        out_shape=jax.ShapeDtypeStruct((b, L, heads * ATT_VD), F32),
        scratch_shapes=[pltpu.VMEM((tk, 2 * tq), F32), pltpu.VMEM((tk, 2 * tq), F32),
                        pltpu.VMEM((1, 2 * tq), F32), pltpu.VMEM((VT_ROWS, 2 * tq), F32)],
        compiler_params=_params("parallel", "parallel", "arbitrary"),
        name="diff_attn_long",
    )(lam_p, g.reshape(1, ATT_VD), qkv, qkv, vt, pre_k, pre_vt)


def _lru_kernel(p_ref, c0_ref, h0_ref, cw_ref, cb_ref, wa_ref, ba_ref, wx_ref, bx_ref, ll_ref,
                o_ref, hout_ref, cout_ref, ext_ref, hc_ref, *, tl, width, conv):
    li = pl.program_id(1)
    pad = 8

    @pl.when(li == 0)
    def _():
        ext_ref[pad - (conv - 1):pad] = c0_ref[0]
        hc_ref[...] = h0_ref[0]

    lx = p_ref[0, :, 0:width]
    lg = p_ref[0, :, width:2 * width]
    ext_ref[pad:pad + tl] = lx
    cw = cw_ref[...]
    xc = cb_ref[...] + cw[conv - 1:conv] * lx
    for s in range(1, conv):
        xc = xc + cw[conv - 1 - s:conv - s] * ext_ref[pad - s:pad - s + tl]
    tail = ext_ref[pad + tl - (conv - 1):pad + tl]
    cout_ref[0] = tail
    ext_ref[pad - (conv - 1):pad] = tail

    xb = xc.astype(BF16)
    r = jax.nn.sigmoid(jnp.dot(xb, wa_ref[...], preferred_element_type=F32) + ba_ref[...])
    i = jax.nn.sigmoid(jnp.dot(xb, wx_ref[...], preferred_element_type=F32) + bx_ref[...])
    z = -ll_ref[...]
    softplus = jnp.maximum(z, 0.0) + jnp.log1p(jnp.exp(-jnp.abs(z)))
    log_a = -LRU_C * r * softplus
    a = jnp.exp(log_a)
    u = jnp.sqrt(-jnp.tanh(log_a) * (1.0 + a * a)) * i * xc

    row = lax.broadcasted_iota(jnp.int32, a.shape, 0)
    d = 1
    while d < tl:
        keep = row >= d
        a_prev = jnp.where(keep, pltpu.roll(a, d, 0), 1.0)
        u_prev = jnp.where(keep, pltpu.roll(u, d, 0), 0.0)
        u = u + a * u_prev
        a = a * a_prev
        d *= 2
    h = u + a * hc_ref[...]
    hc_ref[...] = h[tl - 1:tl]
    hout_ref[0] = h[tl - 1:tl]
    o_ref[0] = h * jax.nn.gelu(lg)


def _lru(proj, c0, h0, cw, cb, wa, ba, wx, bx, ll, tl):
    b, L, _ = proj.shape
    conv, width = cw.shape
    vec = lambda a: a.reshape(1, width)
    const = lambda shape: pl.BlockSpec(shape, lambda bi, li: (0,) * len(shape))
    kern = functools.partial(_lru_kernel, tl=tl, width=width, conv=conv)
    return pl.pallas_call(
        kern,
        grid=(b, L // tl),
        in_specs=[
            pl.BlockSpec((1, tl, 2 * width), lambda bi, li: (bi, li, COL_LRU)),
            pl.BlockSpec((1, conv - 1, width), lambda bi, li: (bi, 0, 0)),
            pl.BlockSpec((1, 1, width), lambda bi, li: (bi, 0, 0)),
            const((conv, width)), const((1, width)),
            const((width, width)), const((1, width)),
            const((width, width)), const((1, width)), const((1, width)),
        ],
        out_specs=[
            pl.BlockSpec((1, tl, width), lambda bi, li: (bi, li, 0)),
            pl.BlockSpec((1, 1, width), lambda bi, li: (bi, 0, 0)),
            pl.BlockSpec((1, conv - 1, width), lambda bi, li: (bi, 0, 0)),
        ],
        out_shape=[jax.ShapeDtypeStruct((b, L, width), F32),
                   jax.ShapeDtypeStruct((b, 1, width), F32),
                   jax.ShapeDtypeStruct((b, conv - 1, width), F32)],
        scratch_shapes=[pltpu.VMEM((tl + 8, width), F32), pltpu.VMEM((1, width), F32)],
        compiler_params=_params("parallel", "arbitrary"),
        name="rg_lru",
    )(proj, c0, h0.reshape(b, 1, width), cw, vec(cb), wa, vec(ba), wx, vec(bx), vec(ll))


def _gla_kernel(pa_ref, pb_ref, s0_ref, w2_ref, gb_ref, gn_ref, o_ref, sout_ref, st_ref, oi_ref,
                *, tl, chunk, kw, vw):
    li = pl.program_id(1)

    @pl.when(li == 0)
    def _():
        st_ref[...] = s0_ref[0]

    q = pa_ref[0, :, 0:kw] * (GLA_DK ** -0.5)
    k = pa_ref[0, :, kw:2 * kw]
    v = pa_ref[0, :, 2 * kw:2 * kw + vw]
    gog = pb_ref[0, :, 0:vw]
    glr = pb_ref[0, :, vw:2 * vw]

    x = jnp.dot(glr.astype(BF16), w2_ref[...], preferred_element_type=F32) + gb_ref[...]
    log_g = (jnp.minimum(x, 0.0) - jnp.log1p(jnp.exp(-jnp.abs(x)))) / GLA_TAU

    rowc = lax.broadcasted_iota(jnp.int32, (tl, kw), 0) % chunk
    bc = log_g
    d = 1
    while d < chunk:
        bc = bc + jnp.where(rowc >= d, pltpu.roll(bc, d, 0), 0.0)
        d *= 2

    hk = lax.broadcasted_iota(jnp.int32, (kw, vw), 0) // GLA_DK
    hv = lax.broadcasted_iota(jnp.int32, (kw, vw), 1) // GLA_DV
    spread = jnp.where(hk == hv, 1.0, 0.0).astype(BF16)

    sub = min(GLA_SUB, chunk)
    row_s = rowc % sub
    o_band = jnp.dot((q * k).astype(BF16), spread, preferred_element_type=F32) * v
    for dl in range(1, sub):
        valid = row_s >= dl
        rel = jnp.where(valid, bc - pltpu.roll(bc, dl, 0), -jnp.inf)
        w = q * pltpu.roll(k, dl, 0) * jnp.exp(rel)
        att = jnp.dot(w.astype(BF16), spread, preferred_element_type=F32)
        o_band = o_band + att * pltpu.roll(v, dl, 0)
    oi_ref[...] = o_band

    n_sub = chunk // sub
    nt_dims = (((1,), (1,)), ((), ()))
    hs = lax.broadcasted_iota(jnp.int32, (vw, kw), 0) // GLA_DV
    hl = lax.broadcasted_iota(jnp.int32, (vw, kw), 1) // GLA_DK
    same_head = hs == hl
    heads = kw // GLA_DK
    if n_sub > 1:
        cat = (n_sub - 1) * kw
        kr = lax.broadcasted_iota(jnp.int32, (heads * chunk, cat), 0) // chunk
        kl = (lax.broadcasted_iota(jnp.int32, (heads * chunk, cat), 1) % kw) // GLA_DK
        key_head = kr == kl
        vr = lax.broadcasted_iota(jnp.int32, (heads * chunk, vw), 0) // chunk
        vl = lax.broadcasted_iota(jnp.int32, (heads * chunk, vw), 1) // GLA_DV
        val_head = vr == vl
        ri = lax.broadcasted_iota(jnp.int32, (chunk, kw), 0)
    for c in range(tl // chunk):
        rows = slice(c * chunk, (c + 1) * chunk)
        b_c = bc[rows]
        q_c = q[rows]
        k_c = k[rows]
        v_c = v[rows].astype(BF16)
        o_c = jnp.zeros((chunk, vw), F32)
        if n_sub > 1:
            qs, ks = [], []
            for i in range(1, n_sub):
                r_i = b_c[i * sub - 1:i * sub]
                in_q = (ri >= i * sub) & (ri < (i + 1) * sub)
                in_k = ri < i * sub
                qs.append(jnp.where(in_q, q_c * jnp.exp(jnp.where(in_q, b_c - r_i, 0.0)), 0.0))
                ks.append(jnp.where(in_k, k_c * jnp.exp(jnp.where(in_k, r_i - b_c, 0.0)), 0.0))
            q_cat = jnp.concatenate(qs, axis=1).astype(BF16)
            k_cat = jnp.concatenate(ks, axis=1).astype(BF16)
            k_bd = jnp.where(key_head, jnp.concatenate([k_cat] * heads, axis=0), jnp.zeros_like(k_cat[:1]))
            att = lax.dot_general(q_cat, k_bd, nt_dims, preferred_element_type=F32)
            v_bd = jnp.where(val_head, jnp.concatenate([v_c] * heads, axis=0), jnp.zeros_like(v_c[:1]))
            o_c = jnp.dot(att.astype(BF16), v_bd, preferred_element_type=F32)
        b_last = b_c[chunk - 1:chunk]
        st = st_ref[...]
        qe = (q_c * jnp.exp(b_c)).astype(BF16)
        o_c = o_c + lax.dot_general(qe, st.astype(BF16), nt_dims, preferred_element_type=F32)
        ke = (k_c * jnp.exp(b_last - b_c)).astype(BF16)
        kv_t = lax.dot_general(v_c, ke, (((0,), (0,)), ((), ())), preferred_element_type=F32)
        st_ref[...] = jnp.exp(b_last) * st + jnp.where(same_head, kv_t, 0.0)
        oi_ref[rows, :] += o_c
    sout_ref[0] = st_ref[...]

    o = oi_ref[...]
    o2 = o * o
    lane_h = lax.broadcasted_iota(jnp.int32, o.shape, 1) // GLA_DV
    ms = jnp.zeros_like(o)
    for h in range(vw // GLA_DV):
        in_h = lane_h == h
        s_h = jnp.sum(jnp.where(in_h, o2, 0.0), axis=-1, keepdims=True) * (1.0 / GLA_DV)
        ms = jnp.where(in_h, s_h, ms)
    o_ref[0] = o * lax.rsqrt(ms + EPS) * gn_ref[...] * jax.nn.silu(gog)


def _gla(proj, s0_t, w2p, gb, gn_t, tl):
    b, L, _ = proj.shape
    vw, kw = s0_t.shape[1:]
    chunk = min(CHUNK, L)
    const = lambda shape: pl.BlockSpec(shape, lambda bi, li: (0,) * len(shape))
    kern = functools.partial(_gla_kernel, tl=tl, chunk=chunk, kw=kw, vw=vw)
    return pl.pallas_call(
        kern,
        grid=(b, L // tl),
        in_specs=[
            pl.BlockSpec((1, tl, PROJ_TN), lambda bi, li: (bi, li, COL_GLA_A)),
            pl.BlockSpec((1, tl, PROJ_TN), lambda bi, li: (bi, li, COL_GLA_B)),
            pl.BlockSpec((1, vw, kw), lambda bi, li: (bi, 0, 0)),
            const((vw, kw)), const((1, kw)), const((1, vw)),
        ],
        out_specs=[
            pl.BlockSpec((1, tl, vw), lambda bi, li: (bi, li, 0)),
            pl.BlockSpec((1, vw, kw), lambda bi, li: (bi, 0, 0)),
        ],
        out_shape=[jax.ShapeDtypeStruct((b, L, vw), F32), jax.ShapeDtypeStruct((b, vw, kw), F32)],
        scratch_shapes=[pltpu.VMEM((vw, kw), F32), pltpu.VMEM((tl, vw), F32)],
        compiler_params=_params("parallel", "arbitrary"),
        name="gla",
    )(proj, proj, s0_t, w2p, gb.reshape(1, kw), gn_t.reshape(1, vw))


def _ffn_kernel(x_ref, oa_ref, ol_ref, og_ref, wo_ref, g_ref, c0_ref, wu_ref, wg_ref, cw_ref, cb_ref,
                wd_ref, gf_ref, out_ref, cout_ref, acc_ref, hn_ref, carry_ref,
                *, bb, tl, tn, conv, nsplit, final_norm):
    li = pl.program_id(1)
    j = pl.program_id(2)
    m = bb * tl
    d = x_ref.shape[-1]
    wa = oa_ref.shape[-1]
    wl = ol_ref.shape[-1]

    @pl.when(j == 0)
    def _():
        mix = jnp.dot(oa_ref[...].reshape(m, wa).astype(BF16), wo_ref[0:wa], preferred_element_type=F32)
        mix += jnp.dot(ol_ref[...].reshape(m, wl).astype(BF16), wo_ref[wa:wa + wl],
                       preferred_element_type=F32)
        mix += jnp.dot(og_ref[...].reshape(m, d - wa - wl).astype(BF16), wo_ref[wa + wl:d],
                       preferred_element_type=F32)
        xn = x_ref[...].reshape(m, d) + mix
        acc_ref[...] = xn
        hn_ref[...] = _rms(xn, g_ref[...]).astype(BF16)

    @pl.when(li == 0)
    def _():
        carry_ref[j] = c0_ref[...]

    prev = carry_ref[j]
    ts = tl // nsplit
    t = lax.broadcasted_iota(jnp.int32, (bb, ts, tn), 1)
    cw = cw_ref[...]
    def up_gate(p):
        hn = hn_ref[p * ts * bb:(p + 1) * ts * bb]
        return (jnp.dot(hn, wu_ref[...], preferred_element_type=F32),
                jnp.dot(hn, wg_ref[...], preferred_element_type=F32))

    ahead = up_gate(0)
    for p in range(nsplit):
        rows = slice(p * ts * bb, (p + 1) * ts * bb)
        u, gate = ahead
        if p + 1 < nsplit:
            ahead = up_gate(p + 1)
        u3 = u.reshape(bb, ts, tn)
        uc = cb_ref[...] + cw[conv - 1:conv] * u3
        for s in range(1, conv):
            shifted = pltpu.roll(u, s, 0).reshape(bb, ts, tn)
            for r in range(s):
                shifted = jnp.where(t == r, prev[:, conv - 1 - s + r:conv - s + r, :], shifted)
            uc = uc + cw[conv - 1 - s:conv - s] * shifted
        prev = u3[:, ts - (conv - 1):ts, :]
        act = (jax.nn.gelu(uc) * gate.reshape(bb, ts, tn)).reshape(bb * ts, tn).astype(BF16)
        acc_ref[rows] += jnp.dot(act, wd_ref[...], preferred_element_type=F32)
    carry_ref[j] = prev
    cout_ref[:, j] = prev

    @pl.when(j == pl.num_programs(2) - 1)
    def _():
        y = acc_ref[...]
        if final_norm:
            y = _rms(y, gf_ref[...])
        out_ref[...] = y.reshape(bb, tl, d)


def _ffn(x, oa, ol, og, wo, g, c0, wu, wg, cw, cb, wd, gf, bb, tl, tn, final_norm):
    b, L, d = x.shape
    conv, f = cw.shape
    nj = f // tn
    nsplit = tl // FFN_PIECE if bb == 1 and tl % FFN_PIECE == 0 else 1
    kern = functools.partial(_ffn_kernel, bb=bb, tl=tl, tn=tn, conv=conv, nsplit=nsplit,
                             final_norm=final_norm)
    act_spec = lambda w: pl.BlockSpec((bb, tl, w), lambda bi, li, j: (bi, li, 0))
    const = lambda shape: pl.BlockSpec(shape, lambda bi, li, j: (0,) * len(shape))
    return pl.pallas_call(
        kern,
        grid=(b // bb, L // tl, nj),
        in_specs=[
            act_spec(d), act_spec(oa.shape[-1]), act_spec(ol.shape[-1]), act_spec(og.shape[-1]),
            const((d, d)), const((1, d)),
            pl.BlockSpec((bb, conv - 1, tn), lambda bi, li, j: (bi, 0, j)),
            pl.BlockSpec((d, tn), lambda bi, li, j: (0, j)),
            pl.BlockSpec((d, tn), lambda bi, li, j: (0, j)),
            pl.BlockSpec((conv, tn), lambda bi, li, j: (0, j)),
            pl.BlockSpec((1, tn), lambda bi, li, j: (0, j)),
            pl.BlockSpec((tn, d), lambda bi, li, j: (j, 0)),
            const((1, d)),
        ],
        out_specs=[
            pl.BlockSpec((bb, tl, d), lambda bi, li, j: (bi, li, 0)),
            pl.BlockSpec((bb, nj, conv - 1, tn), lambda bi, li, j: (bi, 0, 0, 0)),
        ],
        out_shape=[jax.ShapeDtypeStruct((b, L, d), F32), jax.ShapeDtypeStruct((b, nj, conv - 1, tn), F32)],
        scratch_shapes=[pltpu.VMEM((bb * tl, d), F32), pltpu.VMEM((bb * tl, d), BF16),
                        pltpu.VMEM((nj, bb, conv - 1, tn), F32)],
        compiler_params=_params("parallel", "arbitrary", "arbitrary"),
        name="mix_ffn",
    )(x, oa, ol, og, wo, g.reshape(1, d), c0, wu, wg, cw, cb.reshape(1, f), wd, gf.reshape(1, d))


def _block_diag(w):
    n, c, dd = w.shape
    eye = jnp.eye(n, dtype=w.dtype)
    return jnp.einsum('ncd,nm->ncmd', w, eye).reshape(n * c, n * dd)


def _prep_layer(l, w_in, lru_gate_a_w, lru_gate_x_w, gla_gate_w2, gla_norm_g, w_out, ffn_w_up,
                ffn_w_gate, ffn_w_down, sizes):
    att_w, lru_w, gla_kw, gla_vw = sizes
    o = 0
    parts = {}
    for name, wdt in (("q", att_w), ("k", att_w), ("v", att_w), ("lx", lru_w), ("lg", lru_w),
                      ("gq", gla_kw), ("gk", gla_kw), ("gv", gla_vw), ("glr", GLA_RANK), ("gog", gla_vw)):
        parts[name] = w_in[l][:, o:o + wdt]
        o += wdt
    d = w_in.shape[1]
    pad = jnp.zeros((d, PROJ_W - o), w_in.dtype)
    order = ("q", "k", "v", "lx", "lg", "gq", "gk", "gv", "gog", "glr")
    w_in_p = jnp.concatenate([parts[n] for n in order] + [pad], axis=1).astype(BF16)
    w2p = jnp.zeros((gla_vw, gla_kw), F32).at[:GLA_RANK].set(gla_gate_w2[l]).astype(BF16)
    return dict(
        w_in=w_in_p,
        wa=_block_diag(lru_gate_a_w[l]).astype(BF16),
        wx=_block_diag(lru_gate_x_w[l]).astype(BF16),
        w2p=w2p,
        gn_t=jnp.tile(gla_norm_g[l], gla_vw // GLA_DV),
        w_out=w_out[l].astype(BF16),
        wu=ffn_w_up[l].astype(BF16), wg=ffn_w_gate[l].astype(BF16), wd=ffn_w_down[l].astype(BF16),
    )


def _state_to_bd(s):
    b, h, dk, dv = s.shape
    eye = jnp.eye(h, dtype=s.dtype)
    return jnp.einsum('bhde,hg->bhegd', s, eye).reshape(b, h * dv, h * dk)


def _state_from_bd(st, h):
    b, vw, kw = st.shape
    dv, dk = vw // h, kw // h
    s5 = st.reshape(b, h, dv, h, dk)
    diag = jnp.stack([s5[:, i, :, i, :] for i in range(h)], axis=1)
    return diag.transpose(0, 1, 3, 2)


def _pick_tile(n, target):
    t = min(n, target)
    while n % t:
        t //= 2
    return t


def _trunk(x, states, pre_kv, n_pre_valid, layers, prm, heads, gla_heads, tiles, long_attn=False,
           cache=None):
    b, L, d = x.shape
    depth = len(layers)
    tm, tq, tl_lru, tl_gla, bb, tl_ffn, tn = tiles
    outs = []
    for l in range(depth):
        lw = layers[l]
        width = prm['lru_conv_w'].shape[-1]
        f = prm['ffn_conv_w'].shape[-1]
        if states is None:
            h0 = jnp.zeros((b, width), F32)
            c0 = jnp.zeros((b, prm['lru_conv_w'].shape[1] - 1, width), F32)
            s0 = jnp.zeros((b, gla_heads, GLA_DK, GLA_DV), F32)
            fc0 = jnp.zeros((b, prm['ffn_conv_w'].shape[1] - 1, f), F32)
        else:
            h0, c0, s0, fc0 = states[l]
        k32, v32, pf, pb = _in_proj(x.reshape(b * L, d), prm['norm_mix_g'][l], lw['w_in'], tm)
        pf = pf.reshape(b, L, MIX_W)
        pb = pb.reshape(b, L, QKV_W)
        lam_init = 0.8 - 0.6 * math.exp(-0.3 * l)
        lam_p, sub_g = prm['attn_lambda'][l], prm['attn_subln_g'][l]
        if cache is not None:
            o_att = _attention_cached(pb, cache[0], cache[1], l, lam_p, sub_g, lam_init,
                                      _pick_tile(cache[0].shape[2], 1024))
        elif long_attn:
            o_att = _attention_long(pb, pre_kv[l][0], pre_kv[l][1], n_pre_valid, lam_p, sub_g, lam_init,
                                    tq, heads)
        else:
            o_att = _attention(pb, None, None, 0, lam_p, sub_g, lam_init, tq, heads)
        o_lru, h1, c1 = _lru(pf, c0, h0, prm['lru_conv_w'][l], prm['lru_conv_b'][l], lw['wa'],
                             prm['lru_gate_a_b'][l], lw['wx'], prm['lru_gate_x_b'][l],
                             prm['lru_log_lambda'][l], tl_lru)
        o_gla, s1_t = _gla(pf, _state_to_bd(s0), lw['w2p'], prm['gla_gate_b'][l], lw['gn_t'], tl_gla)
        x, fc1 = _ffn(x, o_att, o_lru, o_gla, lw['w_out'], prm['norm_ffn_g'][l], fc0, lw['wu'], lw['wg'],
                      prm['ffn_conv_w'][l], prm['ffn_conv_b'][l], lw['wd'], prm['norm_final_g'],
                      bb, tl_ffn, tn, final_norm=(l == depth - 1))
        fc1 = fc1.transpose(0, 2, 1, 3).reshape(b, fc1.shape[2], f)
        aw = heads * ATT_VD
        outs.append(dict(k=k32.reshape(b, L, heads, ATT_VD), v=v32.reshape(b, L, heads, ATT_VD),
                         kb=pb[:, :, aw:2 * aw],
                         vb=pb[:, :, 2 * aw:3 * aw], h=h1.reshape(b, width), c=c1,
                         s=_state_from_bd(s1_t, gla_heads), fc=fc1))
    return x, outs


def kernel(x_prompt, x_sample, cache_attn_k, cache_attn_v, state_lru_h, state_lru_conv, state_gla, state_ffn_conv, meta_tokens, norm_mix_g, w_in, attn_lambda, attn_subln_g, lru_conv_w, lru_conv_b, lru_gate_a_w, lru_gate_a_b, lru_gate_x_w, lru_gate_x_b, lru_log_lambda, gla_gate_w2, gla_gate_b, gla_norm_g, w_out, norm_ffn_g, ffn_w_up, ffn_conv_w, ffn_conv_b, ffn_w_gate, ffn_w_down, norm_final_g):
    depth = w_in.shape[0]
    b, seq, d = x_prompt.shape
    db, dseq, _ = x_sample.shape
    past = cache_attn_k.shape[2]
    heads = cache_attn_k.shape[3]
    gla_heads = state_gla.shape[2]
    n_meta = meta_tokens.shape[0]
    width = lru_conv_w.shape[-1]
    f = ffn_conv_w.shape[-1]
    att_w = heads * ATT_VD
    sizes = (att_w, width, gla_heads * GLA_DK, gla_heads * GLA_DV)

    prm = dict(norm_mix_g=norm_mix_g, attn_lambda=attn_lambda, attn_subln_g=attn_subln_g,
               lru_conv_w=lru_conv_w, lru_conv_b=lru_conv_b, lru_gate_a_b=lru_gate_a_b,
               lru_gate_x_b=lru_gate_x_b, lru_log_lambda=lru_log_lambda, gla_gate_b=gla_gate_b,
               norm_ffn_g=norm_ffn_g, ffn_conv_w=ffn_conv_w, ffn_conv_b=ffn_conv_b,
               norm_final_g=norm_final_g)
    layers = [_prep_layer(l, w_in, lru_gate_a_w, lru_gate_x_w, gla_gate_w2, gla_norm_g, w_out,
                          ffn_w_up, ffn_w_gate, ffn_w_down, sizes) for l in range(depth)]
    tn = 256 if f % 256 == 0 else 128

    xm = jnp.broadcast_to(meta_tokens.astype(F32)[None], (b, n_meta, d))
    tiles_m = (b * n_meta, n_meta, n_meta, n_meta, b, n_meta, tn)
    _, om = _trunk(xm, None, None, 0, layers, prm, heads, gla_heads, tiles_m)

    pad_rows = 128 - n_meta
    pre_f = [(jnp.pad(o['kb'], ((0, 0), (0, pad_rows), (0, 0))),
              _values_t(jnp.pad(o['vb'], ((0, 0), (0, pad_rows), (0, 0)))
                        .reshape(b, 128, heads, ATT_VD).transpose(0, 2, 1, 3))) for o in om]
    st_f = [(o['h'], o['c'], o['s'], o['fc']) for o in om]
    tiles_f = (_pick_tile(b * seq, 1024), _pick_tile(seq, 1024), _pick_tile(seq, 512),
               _pick_tile(seq, 256), 1, _pick_tile(seq, 512), f // 2)
    yp, of = _trunk(x_prompt, st_f, pre_f, n_meta, layers, prm, heads, gla_heads, tiles_f,
                    long_attn=True)

    st_s = [(state_lru_h[l], state_lru_conv[l], state_gla[l], state_ffn_conv[l]) for l in range(depth)]
    tiles_s = (_pick_tile(db * dseq, 1024), dseq, dseq, dseq, db, dseq, tn)
    ys, os_ = _trunk(x_sample, st_s, None, past, layers, prm, heads, gla_heads, tiles_s,
                     cache=(cache_attn_k, cache_attn_v))

    def stack_p(name, tail_shape):
        return jnp.stack([jnp.concatenate([m[name].reshape((b, n_meta) + tail_shape),
                                           fr[name].reshape((b, seq) + tail_shape)], axis=1)
                          for m, fr in zip(om, of)])

    k_p = stack_p('k', (heads, ATT_VD))
    v_p = stack_p('v', (heads, ATT_VD))
    k_s = jnp.stack([o['k'].reshape(db, dseq, heads, ATT_VD) for o in os_])
    v_s = jnp.stack([o['v'].reshape(db, dseq, heads, ATT_VD) for o in os_])
    st = lambda outs, name: jnp.stack([o[name] for o in outs])
    return (yp, ys, k_p, v_p, st(of, 'h'), st(of, 'c'), st(of, 's'), st(of, 'fc'),
            k_s, v_s, st(os_, 'h'), st(os_, 'c'), st(os_, 's'), st(os_, 'fc'))
```

```python
import functools
import math

import jax
import jax.numpy as jnp
from jax import lax
from jax.experimental import pallas as pl
from jax.experimental.pallas import tpu as pltpu

F32 = jnp.float32
BF16 = jnp.bfloat16

CHUNK = 64
EPS = 1e-6
ATT_DH = 64
ATT_VD = 128
LRU_C = 8.0
GLA_DK = 32
GLA_DV = 64
GLA_RANK = 16
GLA_TAU = 16.0
GLA_SUB = 16
FFN_PIECE = 256

PROJ_TN = 512
PROJ_W = 3072
QKV_W = 1536
MIX_W = PROJ_W - QKV_W
COL_LRU = 0
COL_GLA_A = 1
COL_GLA_B = 2

VMEM_LIMIT = 48 * 1024 * 1024


def _rms(x, g):
    return x * lax.rsqrt(jnp.mean(x * x, axis=-1, keepdims=True) + EPS) * g


def _params(*sem):
    return pltpu.CompilerParams(dimension_semantics=sem, vmem_limit_bytes=VMEM_LIMIT)


def _in_proj_kernel(x_ref, g_ref, w_ref, k_ref, v_ref, mix_ref, ob_ref, hn_ref, *, n_bf_tiles, q_scale):
    j = pl.program_id(1)

    @pl.when(j == 0)
    def _():
        hn_ref[...] = _rms(x_ref[...], g_ref[...]).astype(BF16)

    y = jnp.dot(hn_ref[...], w_ref[...], preferred_element_type=F32)

    @pl.when(j == 1)
    def _():
        for h in range(k_ref.shape[1]):
            k_ref[:, h, :] = y[:, h * 128:(h + 1) * 128]

    @pl.when(j == 2)
    def _():
        for h in range(v_ref.shape[1]):
            v_ref[:, h, :] = y[:, h * 128:(h + 1) * 128]

    @pl.when(j >= n_bf_tiles)
    def _():
        mix_ref[...] = y

    @pl.when(j < n_bf_tiles)
    def _():
        scale = jnp.where(j == 0, q_scale, 1.0).astype(F32)
        ob_ref[...] = (y * scale).astype(BF16)


def _in_proj(x2, g, w, tm):
    m, d = x2.shape
    n_bf = QKV_W // PROJ_TN
    kv_heads = PROJ_TN // 128
    kern = functools.partial(_in_proj_kernel, n_bf_tiles=n_bf, q_scale=ATT_DH ** -0.5 * math.log2(math.e))
    return pl.pallas_call(
        kern,
        grid=(m // tm, PROJ_W // PROJ_TN),
        in_specs=[
            pl.BlockSpec((tm, d), lambda i, j: (i, 0)),
            pl.BlockSpec((1, d), lambda i, j: (0, 0)),
            pl.BlockSpec((d, PROJ_TN), lambda i, j: (0, j)),
        ],
        out_specs=[
            pl.BlockSpec((tm, kv_heads, 128), lambda i, j: (i, 0, 0)),
            pl.BlockSpec((tm, kv_heads, 128), lambda i, j: (i, 0, 0)),
            pl.BlockSpec((tm, PROJ_TN), lambda i, j: (i, jnp.maximum(j - n_bf, 0))),
            pl.BlockSpec((tm, PROJ_TN), lambda i, j: (i, jnp.minimum(j, n_bf - 1))),
        ],
        out_shape=[jax.ShapeDtypeStruct((m, kv_heads, 128), F32), jax.ShapeDtypeStruct((m, kv_heads, 128), F32),
                   jax.ShapeDtypeStruct((m, MIX_W), F32), jax.ShapeDtypeStruct((m, QKV_W), BF16)],
        scratch_shapes=[pltpu.VMEM((tm, d), BF16)],
        compiler_params=_params("parallel", "arbitrary"),
        name="in_proj",
    )(x2, g.reshape(1, d), w)


def _attn_kernel(*refs, tq, tk, n_pre, n_pre_valid, tkp, lam_init, has_pre):
    if has_pre:
        lam_ref, g_ref, q_ref, k_ref, v_ref, kp_ref, vp_ref, o_ref, m_ref, l_ref, acc_ref = refs
    else:
        lam_ref, g_ref, q_ref, k_ref, v_ref, o_ref, m_ref, l_ref, acc_ref = refs
    qi = pl.program_id(2)

    q = q_ref[0]
    lane = lax.broadcasted_iota(jnp.int32, q.shape, 1)
    zero = jnp.zeros_like(q)
    q2 = jnp.concatenate([jnp.where(lane < ATT_DH, q, zero), jnp.where(lane >= ATT_DH, q, zero)], axis=0)

    m_ref[...] = jnp.full(m_ref.shape, -jnp.inf, F32)
    l_ref[...] = jnp.zeros(l_ref.shape, F32)
    acc_ref[...] = jnp.zeros(acc_ref.shape, F32)

    def update(kt, vt, mask):
        s = lax.dot_general(q2, kt.astype(BF16), (((1,), (1,)), ((), ())), preferred_element_type=F32)
        if mask is not None:
            s = jnp.where(mask, s, -jnp.inf)
        m_old = m_ref[...]
        m_new = jnp.maximum(m_old, jnp.max(s, axis=-1, keepdims=True))
        alpha = jnp.exp2(m_old - m_new)
        p = jnp.exp2(s - m_new)
        l_ref[...] = alpha * l_ref[...] + jnp.sum(p, axis=-1, keepdims=True)
        m_ref[...] = m_new
        acc_ref[...] = alpha * acc_ref[...] + jnp.dot(p.astype(BF16), vt.astype(BF16),
                                                      preferred_element_type=F32)

    if has_pre:
        n_tiles = n_pre // tkp
        full_tiles = n_pre_valid // tkp
        if full_tiles > 0:
            def pre_body(t, c):
                off = pl.multiple_of(t * tkp, tkp)
                update(kp_ref[0, pl.ds(off, tkp), :], vp_ref[0, pl.ds(off, tkp), :], None)
                return c
            lax.fori_loop(0, full_tiles, pre_body, 0)
        if full_tiles < n_tiles:
            col = lax.broadcasted_iota(jnp.int32, (2 * tq, tkp), 1)
            off = full_tiles * tkp
            update(kp_ref[0, off:off + tkp, :], vp_ref[0, off:off + tkp, :],
                   col < (n_pre_valid - off))

    def own_body(t, c):
        off = pl.multiple_of(t * tk, tk)
        update(k_ref[0, pl.ds(off, tk), :], v_ref[0, pl.ds(off, tk), :], None)
        return c
    lax.fori_loop(0, qi * (tq // tk), own_body, 0)

    off = pl.multiple_of(qi * tq, tq)
    if tq > CHUNK:
        row = lax.broadcasted_iota(jnp.int32, (2 * tq, tq), 0)
        col = lax.broadcasted_iota(jnp.int32, (2 * tq, tq), 1)
        row = jnp.where(row >= tq, row - tq, row)
        mask = (col // CHUNK) <= (row // CHUNK)
    else:
        mask = None
    update(k_ref[0, pl.ds(off, tq), :], v_ref[0, pl.ds(off, tq), :], mask)

    lq = lam_ref[...]
    lam = (jnp.exp(jnp.sum(lq[0:1] * lq[1:2], axis=-1, keepdims=True))
           - jnp.exp(jnp.sum(lq[2:3] * lq[3:4], axis=-1, keepdims=True)) + lam_init)
    acc = acc_ref[...]
    l = l_ref[...]
    o = acc[:tq] / l[:tq] - lam * (acc[tq:] / l[tq:])
    o_ref[0] = _rms(o, g_ref[...]) * (1.0 - lam_init)


def _attention(qkv, pre_k, pre_v, n_pre_valid, lam_p, g, lam_init, tq, heads):
    b, L, _ = qkv.shape
    has_pre = pre_k is not None
    tk = tq
    kern_kw = dict(tq=tq, tk=tk, lam_init=lam_init, has_pre=has_pre, n_pre=0, n_pre_valid=0, tkp=0)
    in_specs = [
        pl.BlockSpec((4, ATT_DH), lambda bi, h, qi: (0, 0)),
        pl.BlockSpec((1, ATT_VD), lambda bi, h, qi: (0, 0)),
        pl.BlockSpec((1, tq, 128), lambda bi, h, qi: (bi, qi, h)),
        pl.BlockSpec((1, L, 128), lambda bi, h, qi: (bi, 0, heads + h)),
        pl.BlockSpec((1, L, 128), lambda bi, h, qi: (bi, 0, 2 * heads + h)),
    ]
    args = [lam_p, g.reshape(1, ATT_VD), qkv, qkv, qkv]
    if has_pre:
        lp = pre_k.shape[1]
        tkp = min(lp, 512)
        kern_kw.update(n_pre=lp, n_pre_valid=n_pre_valid, tkp=tkp)
        in_specs += [pl.BlockSpec((1, lp, 128), lambda bi, h, qi: (bi, 0, h)),
                     pl.BlockSpec((1, lp, 128), lambda bi, h, qi: (bi, 0, h))]
        args += [pre_k, pre_v]
    return pl.pallas_call(
        functools.partial(_attn_kernel, **kern_kw),
        grid=(b, heads, L // tq),
        in_specs=in_specs,
        out_specs=pl.BlockSpec((1, tq, 128), lambda bi, h, qi: (bi, qi, h)),
        out_shape=jax.ShapeDtypeStruct((b, L, heads * ATT_VD), F32),
        scratch_shapes=[pltpu.VMEM((2 * tq, 1), F32), pltpu.VMEM((2 * tq, 1), F32),
                        pltpu.VMEM((2 * tq, ATT_VD), F32)],
        compiler_params=_params("parallel", "parallel", "arbitrary"),
        name="diff_attn",
    )(*args)


def _attn_cached_kernel(lam_ref, g_ref, qkv_ref, kc_ref, vc_ref, o_ref,
                        q2t_ref, m_ref, l_ref, acc_ref, *, tq, heads, lam_init):
    t = pl.program_id(1)
    nt = (((1,), (1,)), ((), ()))
    tn = (((0,), (0,)), ((), ()))
    w = heads * 128
    lanes = heads * 2 * tq

    @pl.when(t == 0)
    def _():
        r = lax.broadcasted_iota(jnp.int32, (128, 128), 0)
        c = lax.broadcasted_iota(jnp.int32, (128, 128), 1)
        e1 = jnp.where((r == c) & (r < ATT_DH), 1.0, 0.0).astype(BF16)
        e2 = jnp.where((r == c) & (r >= ATT_DH), 1.0, 0.0).astype(BF16)
        parts = []
        for h in range(heads):
            q = qkv_ref[0, :, h * 128:(h + 1) * 128]
            parts += [lax.dot_general(e1, q, nt, preferred_element_type=F32),
                      lax.dot_general(e2, q, nt, preferred_element_type=F32)]
        q2t_ref[...] = jnp.concatenate(parts, axis=1).astype(BF16)
        m_ref[...] = jnp.full(m_ref.shape, -jnp.inf, F32)
        l_ref[...] = jnp.zeros(l_ref.shape, F32)
        acc_ref[...] = jnp.zeros(acc_ref.shape, F32)

    def absorb(kt, vt, row_head):
        s = jnp.dot(kt.astype(BF16), q2t_ref[...], preferred_element_type=F32)
        lane_head = lax.broadcasted_iota(jnp.int32, s.shape, 1) // (2 * tq)
        s = jnp.where(row_head == lane_head, s, -jnp.inf)
        m_old = m_ref[...]
        m_new = jnp.maximum(m_old, jnp.max(s, axis=0, keepdims=True))
        alpha = jnp.exp2(m_old - m_new)
        p = jnp.exp2(s - m_new)
        l_ref[...] = alpha * l_ref[...] + jnp.sum(p, axis=0, keepdims=True)
        m_ref[...] = m_new
        acc_ref[...] = alpha * acc_ref[...] + lax.dot_general(vt.astype(BF16), p.astype(BF16), tn,
                                                              preferred_element_type=F32)

    rows_c = kc_ref.shape[0]
    absorb(kc_ref[...], vc_ref[...], lax.broadcasted_iota(jnp.int32, (rows_c, lanes), 0) % heads)

    @pl.when(t == pl.num_programs(1) - 1)
    def _():
        lq = lam_ref[...]
        lam = (jnp.exp(jnp.sum(lq[0:1] * lq[1:2], axis=-1, keepdims=True))
               - jnp.exp(jnp.sum(lq[2:3] * lq[3:4], axis=-1, keepdims=True)) + lam_init)
        k_own = jnp.concatenate([qkv_ref[0, :, w + h * 128:w + (h + 1) * 128] for h in range(heads)], axis=0)
        v_own = jnp.concatenate([qkv_ref[0, :, 2 * w + h * 128:2 * w + (h + 1) * 128] for h in range(heads)],
                                axis=0)
        absorb(k_own, v_own, lax.broadcasted_iota(jnp.int32, (heads * tq, lanes), 0) // tq)
        n = (acc_ref[...] / l_ref[...]).T
        for h in range(heads):
            o = n[2 * h * tq:(2 * h + 1) * tq] - lam * n[(2 * h + 1) * tq:(2 * h + 2) * tq]
            o_ref[0, :, h * 128:(h + 1) * 128] = _rms(o, g_ref[...]) * (1.0 - lam_init)


def _attention_cached(qkv, cache_k, cache_v, layer, lam_p, g, lam_init, tkc):
    b, tq, _ = qkv.shape
    _, _, past, heads, vd = cache_k.shape
    depth = cache_k.shape[0]
    lanes = heads * 2 * tq
    flat = lambda c: c.reshape(depth, b, past * heads, vd)
    cache_spec = pl.BlockSpec((None, None, tkc * heads, vd), lambda bi, t: (layer, bi, t, 0))
    kern = functools.partial(_attn_cached_kernel, tq=tq, heads=heads, lam_init=lam_init)
    return pl.pallas_call(
        kern,
        grid=(b, past // tkc),
        in_specs=[
            pl.BlockSpec((4, ATT_DH), lambda bi, t: (0, 0)),
            pl.BlockSpec((1, ATT_VD), lambda bi, t: (0, 0)),
            pl.BlockSpec((1, tq, qkv.shape[-1]), lambda bi, t: (bi, 0, 0)),
            cache_spec, cache_spec,
        ],
        out_specs=pl.BlockSpec((1, tq, heads * ATT_VD), lambda bi, t: (bi, 0, 0)),
        out_shape=jax.ShapeDtypeStruct((b, tq, heads * ATT_VD), F32),
        scratch_shapes=[pltpu.VMEM((128, lanes), BF16), pltpu.VMEM((1, lanes), F32),
                        pltpu.VMEM((1, lanes), F32), pltpu.VMEM((ATT_VD, lanes), F32)],
        compiler_params=_params("parallel", "arbitrary"),
        name="diff_attn_cached",
    )(lam_p, g.reshape(1, ATT_VD), qkv, flat(cache_k), flat(cache_v))


def _attn_long_kernel(lam_ref, g_ref, q_ref, k_ref, vt_ref, kp_ref, vtp_ref, o_ref,
                      sa_ref, sb_ref, ta_ref, tb_ref, m_ref, acc_ref, *, tq, tk, n_pre_valid, lam_init):
    qi = pl.program_id(2)
    nt = (((1,), (1,)), ((), ()))

    q = q_ref[0]
    r = lax.broadcasted_iota(jnp.int32, (128, 128), 0)
    c = lax.broadcasted_iota(jnp.int32, (128, 128), 1)
    e1 = jnp.where((r == c) & (r < ATT_DH), 1.0, 0.0).astype(BF16)
    e2 = jnp.where((r == c) & (r >= ATT_DH), 1.0, 0.0).astype(BF16)
    q2t = jnp.concatenate([lax.dot_general(e1, q, nt, preferred_element_type=F32),
                           lax.dot_general(e2, q, nt, preferred_element_type=F32)], axis=1).astype(BF16)

    m_ref[...] = jnp.full(m_ref.shape, -jnp.inf, F32)
    acc_ref[...] = jnp.zeros(acc_ref.shape, F32)

    def scores(kt, mask=None):
        s = jnp.dot(kt, q2t, preferred_element_type=F32)
        return s if mask is None else jnp.where(mask, s, -jnp.inf)

    def scores_to(s_ref, top_ref, t, mask=None):
        off = pl.multiple_of(t * tk, tk)
        s = scores(k_ref[0, pl.ds(off, tk), :], mask)
        s_ref[...] = s
        top_ref[...] = jnp.max(s, axis=0, keepdims=True)

    def absorb(s, top, vt):
        m_old = m_ref[...]
        m_new = jnp.maximum(m_old, top)
        alpha = jnp.exp2(m_old - m_new)
        p = jnp.exp2(s - m_new).astype(BF16)
        m_ref[...] = m_new
        acc_ref[...] = alpha * acc_ref[...] + jnp.dot(vt, p, preferred_element_type=F32)

    def step(cur, nxt, t_next, t, mask=None):
        scores_to(nxt[0], nxt[1], t_next, mask)
        absorb(cur[0][...], cur[1][...], vt_ref[0, 0, t])

    buf_a = (sa_ref, ta_ref)
    buf_b = (sb_ref, tb_ref)

    n_pre = kp_ref.shape[1]
    key = lax.broadcasted_iota(jnp.int32, (n_pre, 2 * tq), 0)
    s_pre = scores(kp_ref[0], key < n_pre_valid)
    absorb(s_pre, jnp.max(s_pre, axis=0, keepdims=True), vtp_ref[0, 0])

    key = lax.broadcasted_iota(jnp.int32, (tk, 2 * tq), 0)
    qry = lax.broadcasted_iota(jnp.int32, (tk, 2 * tq), 1)
    qry = jnp.where(qry >= tq, qry - tq, qry)
    shift = CHUNK.bit_length() - 1
    n_full = 2 * qi
    last = jnp.maximum(n_full - 1, 0)
    scores_to(sa_ref, ta_ref, n_full, (key >> shift) <= (qry >> shift))
    step(buf_a, buf_b, n_full + 1, n_full, ((key + tk) >> shift) <= (qry >> shift))
    step(buf_b, buf_a, 0, n_full + 1)

    def pair(u, carry):
        step(buf_a, buf_b, 2 * u + 1, 2 * u)
        step(buf_b, buf_a, jnp.minimum(2 * u + 2, last), 2 * u + 1)
        return carry
    lax.fori_loop(0, qi, pair, 0)

    lq = lam_ref[...]
    lam = (jnp.exp(jnp.sum(lq[0:1] * lq[1:2], axis=-1, keepdims=True))
           - jnp.exp(jnp.sum(lq[2:3] * lq[3:4], axis=-1, keepdims=True)) + lam_init)
    acc = acc_ref[...]
    num = acc[:ATT_VD]
    den = acc[ATT_VD:ATT_VD + 1]
    o_t = num[:, :tq] / den[:, :tq] - lam * (num[:, tq:] / den[:, tq:])
    o_ref[0] = _rms(o_t.T, g_ref[...]) * (1.0 - lam_init)


VT_ROWS = ATT_VD + 16


def _values_t(v):
    ones = jnp.ones(v.shape[:-1] + (1,), v.dtype)
    zeros = jnp.zeros(v.shape[:-1] + (VT_ROWS - ATT_VD - 1,), v.dtype)
    return jnp.swapaxes(jnp.concatenate([v, ones, zeros], axis=-1), -1, -2)


def _attention_long(qkv, pre_k, pre_vt, n_pre_valid, lam_p, g, lam_init, tq, heads):
    b, L, _ = qkv.shape
    tk = tq // 2
    nt = L // tk
    lp = pre_k.shape[1]
    v = qkv[:, :, 2 * heads * 128:].reshape(b, nt, tk, heads, ATT_VD)
    vt = _values_t(v.transpose(0, 3, 1, 2, 4))
    kern = functools.partial(_attn_long_kernel, tq=tq, tk=tk, n_pre_valid=n_pre_valid, lam_init=lam_init)
    return pl.pallas_call(
        kern,
        grid=(b, heads, L // tq),
        in_specs=[
            pl.BlockSpec((4, ATT_DH), lambda bi, h, qi: (0, 0)),
            pl.BlockSpec((1, ATT_VD), lambda bi, h, qi: (0, 0)),
            pl.BlockSpec((1, tq, 128), lambda bi, h, qi: (bi, qi, h)),
            pl.BlockSpec((1, L, 128), lambda bi, h, qi: (bi, 0, heads + h)),
            pl.BlockSpec((1, 1, nt, VT_ROWS, tk), lambda bi, h, qi: (bi, h, 0, 0, 0)),
            pl.BlockSpec((1, lp, 128), lambda bi, h, qi: (bi, 0, h)),
            pl.BlockSpec((1, 1, VT_ROWS, lp), lambda bi, h, qi: (bi, h, 0, 0)),
        ],
        out_specs=pl.BlockSpec((1, tq, 128), lambda bi, h, qi: (bi, qi, h)),
        out_shape=jax.ShapeDtypeStruct((b, L, heads * ATT_VD), F32),
        scratch_shapes=[pltpu.VMEM((tk, 2 * tq), F32), pltpu.VMEM((tk, 2 * tq), F32),
                        pltpu.VMEM((1, 2 * tq), F32), pltpu.VMEM((1, 2 * tq), F32),
                        pltpu.VMEM((1, 2 * tq), F32), pltpu.VMEM((VT_ROWS, 2 * tq), F32)],
        compiler_params=_params("parallel", "parallel", "arbitrary"),
        name="diff_attn_long",
    )(lam_p, g.reshape(1, ATT_VD), qkv, qkv, vt, pre_k, pre_vt)


def _lru_kernel(p_ref, c0_ref, h0_ref, cw_ref, cb_ref, wa_ref, ba_ref, wx_ref, bx_ref, ll_ref,
                o_ref, hout_ref, cout_ref, ext_ref, hc_ref, *, tl, width, conv):
    li = pl.program_id(1)
    pad = 8

    @pl.when(li == 0)
    def _():
        ext_ref[pad - (conv - 1):pad] = c0_ref[0]
        hc_ref[...] = h0_ref[0]

    lx = p_ref[0, :, 0:width]
    lg = p_ref[0, :, width:2 * width]
    ext_ref[pad:pad + tl] = lx
    cw = cw_ref[...]
    xc = cb_ref[...] + cw[conv - 1:conv] * lx
    for s in range(1, conv):
        xc = xc + cw[conv - 1 - s:conv - s] * ext_ref[pad - s:pad - s + tl]
    tail = ext_ref[pad + tl - (conv - 1):pad + tl]
    cout_ref[0] = tail
    ext_ref[pad - (conv - 1):pad] = tail

    xb = xc.astype(BF16)
    r = jax.nn.sigmoid(jnp.dot(xb, wa_ref[...], preferred_element_type=F32) + ba_ref[...])
    i = jax.nn.sigmoid(jnp.dot(xb, wx_ref[...], preferred_element_type=F32) + bx_ref[...])
    z = -ll_ref[...]
    softplus = jnp.maximum(z, 0.0) + jnp.log1p(jnp.exp(-jnp.abs(z)))
    log_a = -LRU_C * r * softplus
    a = jnp.exp(log_a)
    u = jnp.sqrt(-jnp.tanh(log_a) * (1.0 + a * a)) * i * xc

    row = lax.broadcasted_iota(jnp.int32, a.shape, 0)
    d = 1
    while d < tl:
        keep = row >= d
        a_prev = jnp.where(keep, pltpu.roll(a, d, 0), 1.0)
        u_prev = jnp.where(keep, pltpu.roll(u, d, 0), 0.0)
        u = u + a * u_prev
        a = a * a_prev
        d *= 2
    h = u + a * hc_ref[...]
    hc_ref[...] = h[tl - 1:tl]
    hout_ref[0] = h[tl - 1:tl]
    o_ref[0] = h * jax.nn.gelu(lg)


def _lru(proj, c0, h0, cw, cb, wa, ba, wx, bx, ll, tl):
    b, L, _ = proj.shape
    conv, width = cw.shape
    vec = lambda a: a.reshape(1, width)
    const = lambda shape: pl.BlockSpec(shape, lambda bi, li: (0,) * len(shape))
    kern = functools.partial(_lru_kernel, tl=tl, width=width, conv=conv)
    return pl.pallas_call(
        kern,
        grid=(b, L // tl),
        in_specs=[
            pl.BlockSpec((1, tl, 2 * width), lambda bi, li: (bi, li, COL_LRU)),
            pl.BlockSpec((1, conv - 1, width), lambda bi, li: (bi, 0, 0)),
            pl.BlockSpec((1, 1, width), lambda bi, li: (bi, 0, 0)),
            const((conv, width)), const((1, width)),
            const((width, width)), const((1, width)),
            const((width, width)), const((1, width)), const((1, width)),
        ],
        out_specs=[
            pl.BlockSpec((1, tl, width), lambda bi, li: (bi, li, 0)),
            pl.BlockSpec((1, 1, width), lambda bi, li: (bi, 0, 0)),
            pl.BlockSpec((1, conv - 1, width), lambda bi, li: (bi, 0, 0)),
        ],
        out_shape=[jax.ShapeDtypeStruct((b, L, width), F32),
                   jax.ShapeDtypeStruct((b, 1, width), F32),
                   jax.ShapeDtypeStruct((b, conv - 1, width), F32)],
        scratch_shapes=[pltpu.VMEM((tl + 8, width), F32), pltpu.VMEM((1, width), F32)],
        compiler_params=_params("parallel", "arbitrary"),
        name="rg_lru",
    )(proj, c0, h0.reshape(b, 1, width), cw, vec(cb), wa, vec(ba), wx, vec(bx), vec(ll))


def _gla_kernel(pa_ref, pb_ref, s0_ref, w2_ref, gb_ref, gn_ref, o_ref, sout_ref, st_ref, oi_ref,
                *, tl, chunk, kw, vw):
    li = pl.program_id(1)

    @pl.when(li == 0)
    def _():
        st_ref[...] = s0_ref[0]

    q = pa_ref[0, :, 0:kw] * (GLA_DK ** -0.5)
    k = pa_ref[0, :, kw:2 * kw]
    v = pa_ref[0, :, 2 * kw:2 * kw + vw]
    gog = pb_ref[0, :, 0:vw]
    glr = pb_ref[0, :, vw:2 * vw]

    x = jnp.dot(glr.astype(BF16), w2_ref[...], preferred_element_type=F32) + gb_ref[...]
    log_g = (jnp.minimum(x, 0.0) - jnp.log1p(jnp.exp(-jnp.abs(x)))) / GLA_TAU

    rowc = lax.broadcasted_iota(jnp.int32, (tl, kw), 0) % chunk
    bc = log_g
    d = 1
    while d < chunk:
        bc = bc + jnp.where(rowc >= d, pltpu.roll(bc, d, 0), 0.0)
        d *= 2

    hk = lax.broadcasted_iota(jnp.int32, (kw, vw), 0) // GLA_DK
    hv = lax.broadcasted_iota(jnp.int32, (kw, vw), 1) // GLA_DV
    spread = jnp.where(hk == hv, 1.0, 0.0).astype(BF16)

    sub = min(GLA_SUB, chunk)
    row_s = rowc % sub
    o_band = jnp.dot((q * k).astype(BF16), spread, preferred_element_type=F32) * v
    for dl in range(1, sub):
        valid = row_s >= dl
        rel = jnp.where(valid, bc - pltpu.roll(bc, dl, 0), -jnp.inf)
        w = q * pltpu.roll(k, dl, 0) * jnp.exp(rel)
        att = jnp.dot(w.astype(BF16), spread, preferred_element_type=F32)
        o_band = o_band + att * pltpu.roll(v, dl, 0)
    oi_ref[...] = o_band

    n_sub = chunk // sub
    nt_dims = (((1,), (1,)), ((), ()))
    hs = lax.broadcasted_iota(jnp.int32, (vw, kw), 0) // GLA_DV
    hl = lax.broadcasted_iota(jnp.int32, (vw, kw), 1) // GLA_DK
    same_head = hs == hl
    heads = kw // GLA_DK
    if n_sub > 1:
        cat = (n_sub - 1) * kw
        kr = lax.broadcasted_iota(jnp.int32, (heads * chunk, cat), 0) // chunk
        kl = (lax.broadcasted_iota(jnp.int32, (heads * chunk, cat), 1) % kw) // GLA_DK
        key_head = kr == kl
        vr = lax.broadcasted_iota(jnp.int32, (heads * chunk, vw), 0) // chunk
        vl = lax.broadcasted_iota(jnp.int32, (heads * chunk, vw), 1) // GLA_DV
        val_head = vr == vl
        ri = lax.broadcasted_iota(jnp.int32, (chunk, kw), 0)
    for c in range(tl // chunk):
        rows = slice(c * chunk, (c + 1) * chunk)
        b_c = bc[rows]
        q_c = q[rows]
        k_c = k[rows]
        v_c = v[rows].astype(BF16)
        o_c = jnp.zeros((chunk, vw), F32)
        if n_sub > 1:
            qs, ks = [], []
            for i in range(1, n_sub):
                r_i = b_c[i * sub - 1:i * sub]
                in_q = (ri >= i * sub) & (ri < (i + 1) * sub)
                in_k = ri < i * sub
                qs.append(jnp.where(in_q, q_c * jnp.exp(jnp.where(in_q, b_c - r_i, 0.0)), 0.0))
                ks.append(jnp.where(in_k, k_c * jnp.exp(jnp.where(in_k, r_i - b_c, 0.0)), 0.0))
            q_cat = jnp.concatenate(qs, axis=1).astype(BF16)
            k_cat = jnp.concatenate(ks, axis=1).astype(BF16)
            k_bd = jnp.where(key_head, jnp.concatenate([k_cat] * heads, axis=0), jnp.zeros_like(k_cat[:1]))
            att = lax.dot_general(q_cat, k_bd, nt_dims, preferred_element_type=F32)
            v_bd = jnp.where(val_head, jnp.concatenate([v_c] * heads, axis=0), jnp.zeros_like(v_c[:1]))
            o_c = jnp.dot(att.astype(BF16), v_bd, preferred_element_type=F32)
        b_last = b_c[chunk - 1:chunk]
        st = st_ref[...]
        qe = (q_c * jnp.exp(b_c)).astype(BF16)
        o_c = o_c + lax.dot_general(qe, st.astype(BF16), nt_dims, preferred_element_type=F32)
        ke = (k_c * jnp.exp(b_last - b_c)).astype(BF16)
        kv_t = lax.dot_general(v_c, ke, (((0,), (0,)), ((), ())), preferred_element_type=F32)
        st_ref[...] = jnp.exp(b_last) * st + jnp.where(same_head, kv_t, 0.0)
        oi_ref[rows, :] += o_c
    sout_ref[0] = st_ref[...]

    o = oi_ref[...]
    o2 = o * o
    lane_h = lax.broadcasted_iota(jnp.int32, o.shape, 1) // GLA_DV
    ms = jnp.zeros_like(o)
    for h in range(vw // GLA_DV):
        in_h = lane_h == h
        s_h = jnp.sum(jnp.where(in_h, o2, 0.0), axis=-1, keepdims=True) * (1.0 / GLA_DV)
        ms = jnp.where(in_h, s_h, ms)
    o_ref[0] = o * lax.rsqrt(ms + EPS) * gn_ref[...] * jax.nn.silu(gog)


def _gla(proj, s0_t, w2p, gb, gn_t, tl):
    b, L, _ = proj.shape
    vw, kw = s0_t.shape[1:]
    chunk = min(CHUNK, L)
    const = lambda shape: pl.BlockSpec(shape, lambda bi, li: (0,) * len(shape))
    kern = functools.partial(_gla_kernel, tl=tl, chunk=chunk, kw=kw, vw=vw)
    return pl.pallas_call(
        kern,
        grid=(b, L // tl),
        in_specs=[
            pl.BlockSpec((1, tl, PROJ_TN), lambda bi, li: (bi, li, COL_GLA_A)),
            pl.BlockSpec((1, tl, PROJ_TN), lambda bi, li: (bi, li, COL_GLA_B)),
            pl.BlockSpec((1, vw, kw), lambda bi, li: (bi, 0, 0)),
            const((vw, kw)), const((1, kw)), const((1, vw)),
        ],
        out_specs=[
            pl.BlockSpec((1, tl, vw), lambda bi, li: (bi, li, 0)),
            pl.BlockSpec((1, vw, kw), lambda bi, li: (bi, 0, 0)),
        ],
        out_shape=[jax.ShapeDtypeStruct((b, L, vw), F32), jax.ShapeDtypeStruct((b, vw, kw), F32)],
        scratch_shapes=[pltpu.VMEM((vw, kw), F32), pltpu.VMEM((tl, vw), F32)],
        compiler_params=_params("parallel", "arbitrary"),
        name="gla",
    )(proj, proj, s0_t, w2p, gb.reshape(1, kw), gn_t.reshape(1, vw))


def _ffn_kernel(x_ref, oa_ref, ol_ref, og_ref, wo_ref, g_ref, c0_ref, wu_ref, wg_ref, cw_ref, cb_ref,
                wd_ref, gf_ref, out_ref, cout_ref, acc_ref, hn_ref, carry_ref,
                *, bb, tl, tn, conv, nsplit, final_norm):
    li = pl.program_id(1)
    j = pl.program_id(2)
    m = bb * tl
    d = x_ref.shape[-1]
    wa = oa_ref.shape[-1]
    wl = ol_ref.shape[-1]

    @pl.when(j == 0)
    def _():
        mix = jnp.dot(oa_ref[...].reshape(m, wa).astype(BF16), wo_ref[0:wa], preferred_element_type=F32)
        mix += jnp.dot(ol_ref[...].reshape(m, wl).astype(BF16), wo_ref[wa:wa + wl],
                       preferred_element_type=F32)
        mix += jnp.dot(og_ref[...].reshape(m, d - wa - wl).astype(BF16), wo_ref[wa + wl:d],
                       preferred_element_type=F32)
        xn = x_ref[...].reshape(m, d) + mix
        acc_ref[...] = xn
        hn_ref[...] = _rms(xn, g_ref[...]).astype(BF16)

    @pl.when(li == 0)
    def _():
        carry_ref[j] = c0_ref[...]

    prev = carry_ref[j]
    ts = tl // nsplit
    t = lax.broadcasted_iota(jnp.int32, (bb, ts, tn), 1)
    cw = cw_ref[...]
    def up_gate(p):
        hn = hn_ref[p * ts * bb:(p + 1) * ts * bb]
        return (jnp.dot(hn, wu_ref[...], preferred_element_type=F32),
                jnp.dot(hn, wg_ref[...], preferred_element_type=F32))

    ahead = up_gate(0)
    for p in range(nsplit):
        rows = slice(p * ts * bb, (p + 1) * ts * bb)
        u, gate = ahead
        if p + 1 < nsplit:
            ahead = up_gate(p + 1)
        u3 = u.reshape(bb, ts, tn)
        uc = cb_ref[...] + cw[conv - 1:conv] * u3
        for s in range(1, conv):
            shifted = pltpu.roll(u, s, 0).reshape(bb, ts, tn)
            for r in range(s):
                shifted = jnp.where(t == r, prev[:, conv - 1 - s + r:conv - s + r, :], shifted)
            uc = uc + cw[conv - 1 - s:conv - s] * shifted
        prev = u3[:, ts - (conv - 1):ts, :]
        act = (jax.nn.gelu(uc) * gate.reshape(bb, ts, tn)).reshape(bb * ts, tn).astype(BF16)
        acc_ref[rows] += jnp.dot(act, wd_ref[...], preferred_element_type=F32)
    carry_ref[j] = prev
    cout_ref[:, j] = prev

    @pl.when(j == pl.num_programs(2) - 1)
    def _():
        y = acc_ref[...]
        if final_norm:
            y = _rms(y, gf_ref[...])
        out_ref[...] = y.reshape(bb, tl, d)


def _ffn(x, oa, ol, og, wo, g, c0, wu, wg, cw, cb, wd, gf, bb, tl, tn, final_norm):
    b, L, d = x.shape
    conv, f = cw.shape
    nj = f // tn
    nsplit = tl // FFN_PIECE if bb == 1 and tl % FFN_PIECE == 0 else 1
    kern = functools.partial(_ffn_kernel, bb=bb, tl=tl, tn=tn, conv=conv, nsplit=nsplit,
                             final_norm=final_norm)
    act_spec = lambda w: pl.BlockSpec((bb, tl, w), lambda bi, li, j: (bi, li, 0))
    const = lambda shape: pl.BlockSpec(shape, lambda bi, li, j: (0,) * len(shape))
    return pl.pallas_call(
        kern,
        grid=(b // bb, L // tl, nj),
        in_specs=[
            act_spec(d), act_spec(oa.shape[-1]), act_spec(ol.shape[-1]), act_spec(og.shape[-1]),
            const((d, d)), const((1, d)),
            pl.BlockSpec((bb, conv - 1, tn), lambda bi, li, j: (bi, 0, j)),
            pl.BlockSpec((d, tn), lambda bi, li, j: (0, j)),
            pl.BlockSpec((d, tn), lambda bi, li, j: (0, j)),
            pl.BlockSpec((conv, tn), lambda bi, li, j: (0, j)),
            pl.BlockSpec((1, tn), lambda bi, li, j: (0, j)),
            pl.BlockSpec((tn, d), lambda bi, li, j: (j, 0)),
            const((1, d)),
        ],
        out_specs=[
            pl.BlockSpec((bb, tl, d), lambda bi, li, j: (bi, li, 0)),
            pl.BlockSpec((bb, nj, conv - 1, tn), lambda bi, li, j: (bi, 0, 0, 0)),
        ],
        out_shape=[jax.ShapeDtypeStruct((b, L, d), F32), jax.ShapeDtypeStruct((b, nj, conv - 1, tn), F32)],
        scratch_shapes=[pltpu.VMEM((bb * tl, d), F32), pltpu.VMEM((bb * tl, d), BF16),
                        pltpu.VMEM((nj, bb, conv - 1, tn), F32)],
        compiler_params=_params("parallel", "arbitrary", "arbitrary"),
        name="mix_ffn",
    )(x, oa, ol, og, wo, g.reshape(1, d), c0, wu, wg, cw, cb.reshape(1, f), wd, gf.reshape(1, d))


def _block_diag(w):
    n, c, dd = w.shape
    eye = jnp.eye(n, dtype=w.dtype)
    return jnp.einsum('ncd,nm->ncmd', w, eye).reshape(n * c, n * dd)


def _prep_layer(l, w_in, lru_gate_a_w, lru_gate_x_w, gla_gate_w2, gla_norm_g, w_out, ffn_w_up,
                ffn_w_gate, ffn_w_down, sizes):
    att_w, lru_w, gla_kw, gla_vw = sizes
    o = 0
    parts = {}
    for name, wdt in (("q", att_w), ("k", att_w), ("v", att_w), ("lx", lru_w), ("lg", lru_w),
                      ("gq", gla_kw), ("gk", gla_kw), ("gv", gla_vw), ("glr", GLA_RANK), ("gog", gla_vw)):
        parts[name] = w_in[l][:, o:o + wdt]
        o += wdt
    d = w_in.shape[1]
    pad = jnp.zeros((d, PROJ_W - o), w_in.dtype)
    order = ("q", "k", "v", "lx", "lg", "gq", "gk", "gv", "gog", "glr")
    w_in_p = jnp.concatenate([parts[n] for n in order] + [pad], axis=1).astype(BF16)
    w2p = jnp.zeros((gla_vw, gla_kw), F32).at[:GLA_RANK].set(gla_gate_w2[l]).astype(BF16)
    return dict(
        w_in=w_in_p,
        wa=_block_diag(lru_gate_a_w[l]).astype(BF16),
        wx=_block_diag(lru_gate_x_w[l]).astype(BF16),
        w2p=w2p,
        gn_t=jnp.tile(gla_norm_g[l], gla_vw // GLA_DV),
        w_out=w_out[l].astype(BF16),
        wu=ffn_w_up[l].astype(BF16), wg=ffn_w_gate[l].astype(BF16), wd=ffn_w_down[l].astype(BF16),
    )


def _state_to_bd(s):
    b, h, dk, dv = s.shape
    eye = jnp.eye(h, dtype=s.dtype)
    return jnp.einsum('bhde,hg->bhegd', s, eye).reshape(b, h * dv, h * dk)


def _state_from_bd(st, h):
    b, vw, kw = st.shape
    dv, dk = vw // h, kw // h
    s5 = st.reshape(b, h, dv, h, dk)
    diag = jnp.stack([s5[:, i, :, i, :] for i in range(h)], axis=1)
    return diag.transpose(0, 1, 3, 2)


def _pick_tile(n, target):
    t = min(n, target)
    while n % t:
        t //= 2
    return t


def _trunk(x, states, pre_kv, n_pre_valid, layers, prm, heads, gla_heads, tiles, long_attn=False,
           cache=None):
    b, L, d = x.shape
    depth = len(layers)
    tm, tq, tl_lru, tl_gla, bb, tl_ffn, tn = tiles
    outs = []
    for l in range(depth):
        lw = layers[l]
        width = prm['lru_conv_w'].shape[-1]
        f = prm['ffn_conv_w'].shape[-1]
        if states is None:
            h0 = jnp.zeros((b, width), F32)
            c0 = jnp.zeros((b, prm['lru_conv_w'].shape[1] - 1, width), F32)
            s0 = jnp.zeros((b, gla_heads, GLA_DK, GLA_DV), F32)
            fc0 = jnp.zeros((b, prm['ffn_conv_w'].shape[1] - 1, f), F32)
        else:
            h0, c0, s0, fc0 = states[l]
        k32, v32, pf, pb = _in_proj(x.reshape(b * L, d), prm['norm_mix_g'][l], lw['w_in'], tm)
        pf = pf.reshape(b, L, MIX_W)
        pb = pb.reshape(b, L, QKV_W)
        lam_init = 0.8 - 0.6 * math.exp(-0.3 * l)
        lam_p, sub_g = prm['attn_lambda'][l], prm['attn_subln_g'][l]
        if cache is not None:
            o_att = _attention_cached(pb, cache[0], cache[1], l, lam_p, sub_g, lam_init,
                                      _pick_tile(cache[0].shape[2], 1024))
        elif long_attn:
            o_att = _attention_long(pb, pre_kv[l][0], pre_kv[l][1], n_pre_valid, lam_p, sub_g, lam_init,
                                    tq, heads)
        else:
            o_att = _attention(pb, None, None, 0, lam_p, sub_g, lam_init, tq, heads)
        o_lru, h1, c1 = _lru(pf, c0, h0, prm['lru_conv_w'][l], prm['lru_conv_b'][l], lw['wa'],
                             prm['lru_gate_a_b'][l], lw['wx'], prm['lru_gate_x_b'][l],
                             prm['lru_log_lambda'][l], tl_lru)
        o_gla, s1_t = _gla(pf, _state_to_bd(s0), lw['w2p'], prm['gla_gate_b'][l], lw['gn_t'], tl_gla)
        x, fc1 = _ffn(x, o_att, o_lru, o_gla, lw['w_out'], prm['norm_ffn_g'][l], fc0, lw['wu'], lw['wg'],
                      prm['ffn_conv_w'][l], prm['ffn_conv_b'][l], lw['wd'], prm['norm_final_g'],
                      bb, tl_ffn, tn, final_norm=(l == depth - 1))
        fc1 = fc1.transpose(0, 2, 1, 3).reshape(b, fc1.shape[2], f)
        aw = heads * ATT_VD
        outs.append(dict(k=k32.reshape(b, L, heads, ATT_VD), v=v32.reshape(b, L, heads, ATT_VD),
                         kb=pb[:, :, aw:2 * aw],
                         vb=pb[:, :, 2 * aw:3 * aw], h=h1.reshape(b, width), c=c1,
                         s=_state_from_bd(s1_t, gla_heads), fc=fc1))
    return x, outs


def kernel(x_prompt, x_sample, cache_attn_k, cache_attn_v, state_lru_h, state_lru_conv, state_gla, state_ffn_conv, meta_tokens, norm_mix_g, w_in, attn_lambda, attn_subln_g, lru_conv_w, lru_conv_b, lru_gate_a_w, lru_gate_a_b, lru_gate_x_w, lru_gate_x_b, lru_log_lambda, gla_gate_w2, gla_gate_b, gla_norm_g, w_out, norm_ffn_g, ffn_w_up, ffn_conv_w, ffn_conv_b, ffn_w_gate, ffn_w_down, norm_final_g):
    depth = w_in.shape[0]
    b, seq, d = x_prompt.shape
    db, dseq, _ = x_sample.shape
    past = cache_attn_k.shape[2]
    heads = cache_attn_k.shape[3]
    gla_heads = state_gla.shape[2]
    n_meta = meta_tokens.shape[0]
    width = lru_conv_w.shape[-1]
    f = ffn_conv_w.shape[-1]
    att_w = heads * ATT_VD
    sizes = (att_w, width, gla_heads * GLA_DK, gla_heads * GLA_DV)

    prm = dict(norm_mix_g=norm_mix_g, attn_lambda=attn_lambda, attn_subln_g=attn_subln_g,
               lru_conv_w=lru_conv_w, lru_conv_b=lru_conv_b, lru_gate_a_b=lru_gate_a_b,
               lru_gate_x_b=lru_gate_x_b, lru_log_lambda=lru_log_lambda, gla_gate_b=gla_gate_b,
               norm_ffn_g=norm_ffn_g, ffn_conv_w=ffn_conv_w, ffn_conv_b=ffn_conv_b,
               norm_final_g=norm_final_g)
    layers = [_prep_layer(l, w_in, lru_gate_a_w, lru_gate_x_w, gla_gate_w2, gla_norm_g, w_out,
                          ffn_w_up, ffn_w_gate, ffn_w_down, sizes) for l in range(depth)]
    tn = 256 if f % 256 == 0 else 128

    xm = jnp.broadcast_to(meta_tokens.astype(F32)[None], (b, n_meta, d))
    tiles_m = (b * n_meta, n_meta, n_meta, n_meta, b, n_meta, tn)
    _, om = _trunk(xm, None, None, 0, layers, prm, heads, gla_heads, tiles_m)

    pad_rows = 128 - n_meta
    pre_f = [(jnp.pad(o['kb'], ((0, 0), (0, pad_rows), (0, 0))),
              _values_t(jnp.pad(o['vb'], ((0, 0), (0, pad_rows), (0, 0)))
                        .reshape(b, 128, heads, ATT_VD).transpose(0, 2, 1, 3))) for o in om]
    st_f = [(o['h'], o['c'], o['s'], o['fc']) for o in om]
    tiles_f = (_pick_tile(b * seq, 1024), _pick_tile(seq, 1024), _pick_tile(seq, 512),
               _pick_tile(seq, 256), 1, _pick_tile(seq, 512), f // 2)
    yp, of = _trunk(x_prompt, st_f, pre_f, n_meta, layers, prm, heads, gla_heads, tiles_f,
                    long_attn=True)

    st_s = [(state_lru_h[l], state_lru_conv[l], state_gla[l], state_ffn_conv[l]) for l in range(depth)]
    tiles_s = (_pick_tile(db * dseq, 1024), dseq, dseq, dseq, db, dseq, tn)
    ys, os_ = _trunk(x_sample, st_s, None, past, layers, prm, heads, gla_heads, tiles_s,
                     cache=(cache_attn_k, cache_attn_v))

    def stack_p(name, tail_shape):
        return jnp.stack([jnp.concatenate([m[name].reshape((b, n_meta) + tail_shape),
                                           fr[name].reshape((b, seq) + tail_shape)], axis=1)
                          for m, fr in zip(om, of)])

    k_p = stack_p('k', (heads, ATT_VD))
    v_p = stack_p('v', (heads, ATT_VD))
    k_s = jnp.stack([o['k'].reshape(db, dseq, heads, ATT_VD) for o in os_])
    v_s = jnp.stack([o['v'].reshape(db, dseq, heads, ATT_VD) for o in os_])
    st = lambda outs, name: jnp.stack([o[name] for o in outs])
    return (yp, ys, k_p, v_p, st(of, 'h'), st(of, 'c'), st(of, 's'), st(of, 'fc'),
            k_s, v_s, st(os_, 'h'), st(os_, 'c'), st(os_, 's'), st(os_, 'fc'))
```

```python
import functools
import math

import jax
import jax.numpy as jnp
from jax import lax
from jax.experimental import pallas as pl
from jax.experimental.pallas import tpu as pltpu

F32 = jnp.float32
BF16 = jnp.bfloat16

CHUNK = 64
EPS = 1e-6
ATT_DH = 64
ATT_VD = 128
LRU_C = 8.0
GLA_DK = 32
GLA_DV = 64
GLA_RANK = 16
GLA_TAU = 16.0
GLA_SUB = 16
FFN_PIECE = 256

PROJ_TN = 512
PROJ_W = 3072
QKV_W = 1536
MIX_W = PROJ_W - QKV_W
COL_LRU = 0
COL_GLA_A = 1
COL_GLA_B = 2

VMEM_LIMIT = 48 * 1024 * 1024
FFN_VMEM_LIMIT = 58 * 1024 * 1024


def _rms(x, g):
    return x * lax.rsqrt(jnp.mean(x * x, axis=-1, keepdims=True) + EPS) * g


def _params(*sem):
    return pltpu.CompilerParams(dimension_semantics=sem, vmem_limit_bytes=VMEM_LIMIT)


def _in_proj_kernel(x_ref, g_ref, w_ref, k_ref, v_ref, mix_ref, ob_ref, hn_ref, *, n_bf_tiles, q_scale):
    j = pl.program_id(1)

    @pl.when(j == 0)
    def _():
        hn_ref[...] = _rms(x_ref[...], g_ref[...]).astype(BF16)

    y = jnp.dot(hn_ref[...], w_ref[...], preferred_element_type=F32)

    @pl.when(j == 1)
    def _():
        for h in range(k_ref.shape[1]):
            k_ref[:, h, :] = y[:, h * 128:(h + 1) * 128]

    @pl.when(j == 2)
    def _():
        for h in range(v_ref.shape[1]):
            v_ref[:, h, :] = y[:, h * 128:(h + 1) * 128]

    @pl.when(j >= n_bf_tiles)
    def _():
        mix_ref[...] = y

    @pl.when(j < n_bf_tiles)
    def _():
        scale = jnp.where(j == 0, q_scale, 1.0).astype(F32)
        ob_ref[...] = (y * scale).astype(BF16)


def _in_proj(x2, g, w, tm):
    m, d = x2.shape
    n_bf = QKV_W // PROJ_TN
    kv_heads = PROJ_TN // 128
    kern = functools.partial(_in_proj_kernel, n_bf_tiles=n_bf, q_scale=ATT_DH ** -0.5 * math.log2(math.e))
    return pl.pallas_call(
        kern,
        grid=(m // tm, PROJ_W // PROJ_TN),
        in_specs=[
            pl.BlockSpec((tm, d), lambda i, j: (i, 0)),
            pl.BlockSpec((1, d), lambda i, j: (0, 0)),
            pl.BlockSpec((d, PROJ_TN), lambda i, j: (0, j)),
        ],
        out_specs=[
            pl.BlockSpec((tm, kv_heads, 128), lambda i, j: (i, 0, 0)),
            pl.BlockSpec((tm, kv_heads, 128), lambda i, j: (i, 0, 0)),
            pl.BlockSpec((tm, PROJ_TN), lambda i, j: (i, jnp.maximum(j - n_bf, 0))),
            pl.BlockSpec((tm, PROJ_TN), lambda i, j: (i, jnp.minimum(j, n_bf - 1))),
        ],
        out_shape=[jax.ShapeDtypeStruct((m, kv_heads, 128), F32), jax.ShapeDtypeStruct((m, kv_heads, 128), F32),
                   jax.ShapeDtypeStruct((m, MIX_W), F32), jax.ShapeDtypeStruct((m, QKV_W), BF16)],
        scratch_shapes=[pltpu.VMEM((tm, d), BF16)],
        compiler_params=_params("parallel", "arbitrary"),
        name="in_proj",
    )(x2, g.reshape(1, d), w)


def _attn_kernel(*refs, tq, tk, n_pre, n_pre_valid, tkp, lam_init, has_pre):
    if has_pre:
        lam_ref, g_ref, q_ref, k_ref, v_ref, kp_ref, vp_ref, o_ref, m_ref, l_ref, acc_ref = refs
    else:
        lam_ref, g_ref, q_ref, k_ref, v_ref, o_ref, m_ref, l_ref, acc_ref = refs
    qi = pl.program_id(2)

    q = q_ref[0]
    lane = lax.broadcasted_iota(jnp.int32, q.shape, 1)
    zero = jnp.zeros_like(q)
    q2 = jnp.concatenate([jnp.where(lane < ATT_DH, q, zero), jnp.where(lane >= ATT_DH, q, zero)], axis=0)

    m_ref[...] = jnp.full(m_ref.shape, -jnp.inf, F32)
    l_ref[...] = jnp.zeros(l_ref.shape, F32)
    acc_ref[...] = jnp.zeros(acc_ref.shape, F32)

    def update(kt, vt, mask):
        s = lax.dot_general(q2, kt.astype(BF16), (((1,), (1,)), ((), ())), preferred_element_type=F32)
        if mask is not None:
            s = jnp.where(mask, s, -jnp.inf)
        m_old = m_ref[...]
        m_new = jnp.maximum(m_old, jnp.max(s, axis=-1, keepdims=True))
        alpha = jnp.exp2(m_old - m_new)
        p = jnp.exp2(s - m_new)
        l_ref[...] = alpha * l_ref[...] + jnp.sum(p, axis=-1, keepdims=True)
        m_ref[...] = m_new
        acc_ref[...] = alpha * acc_ref[...] + jnp.dot(p.astype(BF16), vt.astype(BF16),
                                                      preferred_element_type=F32)

    if has_pre:
        n_tiles = n_pre // tkp
        full_tiles = n_pre_valid // tkp
        if full_tiles > 0:
            def pre_body(t, c):
                off = pl.multiple_of(t * tkp, tkp)
                update(kp_ref[0, pl.ds(off, tkp), :], vp_ref[0, pl.ds(off, tkp), :], None)
                return c
            lax.fori_loop(0, full_tiles, pre_body, 0)
        if full_tiles < n_tiles:
            col = lax.broadcasted_iota(jnp.int32, (2 * tq, tkp), 1)
            off = full_tiles * tkp
            update(kp_ref[0, off:off + tkp, :], vp_ref[0, off:off + tkp, :],
                   col < (n_pre_valid - off))

    def own_body(t, c):
        off = pl.multiple_of(t * tk, tk)
        update(k_ref[0, pl.ds(off, tk), :], v_ref[0, pl.ds(off, tk), :], None)
        return c
    lax.fori_loop(0, qi * (tq // tk), own_body, 0)

    off = pl.multiple_of(qi * tq, tq)
    if tq > CHUNK:
        row = lax.broadcasted_iota(jnp.int32, (2 * tq, tq), 0)
        col = lax.broadcasted_iota(jnp.int32, (2 * tq, tq), 1)
        row = jnp.where(row >= tq, row - tq, row)
        mask = (col // CHUNK) <= (row // CHUNK)
    else:
        mask = None
    update(k_ref[0, pl.ds(off, tq), :], v_ref[0, pl.ds(off, tq), :], mask)

    lq = lam_ref[...]
    lam = (jnp.exp(jnp.sum(lq[0:1] * lq[1:2], axis=-1, keepdims=True))
           - jnp.exp(jnp.sum(lq[2:3] * lq[3:4], axis=-1, keepdims=True)) + lam_init)
    acc = acc_ref[...]
    l = l_ref[...]
    o = acc[:tq] / l[:tq] - lam * (acc[tq:] / l[tq:])
    o_ref[0] = _rms(o, g_ref[...]) * (1.0 - lam_init)


def _attention(qkv, pre_k, pre_v, n_pre_valid, lam_p, g, lam_init, tq, heads):
    b, L, _ = qkv.shape
    has_pre = pre_k is not None
    tk = tq
    kern_kw = dict(tq=tq, tk=tk, lam_init=lam_init, has_pre=has_pre, n_pre=0, n_pre_valid=0, tkp=0)
    in_specs = [
        pl.BlockSpec((4, ATT_DH), lambda bi, h, qi: (0, 0)),
        pl.BlockSpec((1, ATT_VD), lambda bi, h, qi: (0, 0)),
        pl.BlockSpec((1, tq, 128), lambda bi, h, qi: (bi, qi, h)),
        pl.BlockSpec((1, L, 128), lambda bi, h, qi: (bi, 0, heads + h)),
        pl.BlockSpec((1, L, 128), lambda bi, h, qi: (bi, 0, 2 * heads + h)),
    ]
    args = [lam_p, g.reshape(1, ATT_VD), qkv, qkv, qkv]
    if has_pre:
        lp = pre_k.shape[1]
        tkp = min(lp, 512)
        kern_kw.update(n_pre=lp, n_pre_valid=n_pre_valid, tkp=tkp)
        in_specs += [pl.BlockSpec((1, lp, 128), lambda bi, h, qi: (bi, 0, h)),
                     pl.BlockSpec((1, lp, 128), lambda bi, h, qi: (bi, 0, h))]
        args += [pre_k, pre_v]
    return pl.pallas_call(
        functools.partial(_attn_kernel, **kern_kw),
        grid=(b, heads, L // tq),
        in_specs=in_specs,
        out_specs=pl.BlockSpec((1, tq, 128), lambda bi, h, qi: (bi, qi, h)),
        out_shape=jax.ShapeDtypeStruct((b, L, heads * ATT_VD), F32),
        scratch_shapes=[pltpu.VMEM((2 * tq, 1), F32), pltpu.VMEM((2 * tq, 1), F32),
                        pltpu.VMEM((2 * tq, ATT_VD), F32)],
        compiler_params=_params("parallel", "parallel", "arbitrary"),
        name="diff_attn",
    )(*args)


def _attn_cached_kernel(lam_ref, g_ref, qkv_ref, kc_ref, vc_ref, o_ref,
                        q2t_ref, m_ref, l_ref, acc_ref, *, tq, heads, lam_init):
    t = pl.program_id(1)
    nt = (((1,), (1,)), ((), ()))
    tn = (((0,), (0,)), ((), ()))
    w = heads * 128
    lanes = heads * 2 * tq

    @pl.when(t == 0)
    def _():
        r = lax.broadcasted_iota(jnp.int32, (128, 128), 0)
        c = lax.broadcasted_iota(jnp.int32, (128, 128), 1)
        e1 = jnp.where((r == c) & (r < ATT_DH), 1.0, 0.0).astype(BF16)
        e2 = jnp.where((r == c) & (r >= ATT_DH), 1.0, 0.0).astype(BF16)
        parts = []
        for h in range(heads):
            q = qkv_ref[0, :, h * 128:(h + 1) * 128]
            parts += [lax.dot_general(e1, q, nt, preferred_element_type=F32),
                      lax.dot_general(e2, q, nt, preferred_element_type=F32)]
        q2t_ref[...] = jnp.concatenate(parts, axis=1).astype(BF16)
        m_ref[...] = jnp.full(m_ref.shape, -jnp.inf, F32)
        l_ref[...] = jnp.zeros(l_ref.shape, F32)
        acc_ref[...] = jnp.zeros(acc_ref.shape, F32)

    def absorb(kt, vt, row_head):
        s = jnp.dot(kt.astype(BF16), q2t_ref[...], preferred_element_type=F32)
        lane_head = lax.broadcasted_iota(jnp.int32, s.shape, 1) // (2 * tq)
        s = jnp.where(row_head == lane_head, s, -jnp.inf)
        m_old = m_ref[...]
        m_new = jnp.maximum(m_old, jnp.max(s, axis=0, keepdims=True))
        alpha = jnp.exp2(m_old - m_new)
        p = jnp.exp2(s - m_new)
        l_ref[...] = alpha * l_ref[...] + jnp.sum(p, axis=0, keepdims=True)
        m_ref[...] = m_new
        acc_ref[...] = alpha * acc_ref[...] + lax.dot_general(vt.astype(BF16), p.astype(BF16), tn,
                                                              preferred_element_type=F32)

    rows_c = kc_ref.shape[0]
    absorb(kc_ref[...], vc_ref[...], lax.broadcasted_iota(jnp.int32, (rows_c, lanes), 0) % heads)

    @pl.when(t == pl.num_programs(1) - 1)
    def _():
        lq = lam_ref[...]
        lam = (jnp.exp(jnp.sum(lq[0:1] * lq[1:2], axis=-1, keepdims=True))
               - jnp.exp(jnp.sum(lq[2:3] * lq[3:4], axis=-1, keepdims=True)) + lam_init)
        k_own = jnp.concatenate([qkv_ref[0, :, w + h * 128:w + (h + 1) * 128] for h in range(heads)], axis=0)
        v_own = jnp.concatenate([qkv_ref[0, :, 2 * w + h * 128:2 * w + (h + 1) * 128] for h in range(heads)],
                                axis=0)
        absorb(k_own, v_own, lax.broadcasted_iota(jnp.int32, (heads * tq, lanes), 0) // tq)
        n = (acc_ref[...] / l_ref[...]).T
        for h in range(heads):
            o = n[2 * h * tq:(2 * h + 1) * tq] - lam * n[(2 * h + 1) * tq:(2 * h + 2) * tq]
            o_ref[0, :, h * 128:(h + 1) * 128] = _rms(o, g_ref[...]) * (1.0 - lam_init)


def _attention_cached(qkv, cache_k, cache_v, layer, lam_p, g, lam_init, tkc):
    b, tq, _ = qkv.shape
    _, _, past, heads, vd = cache_k.shape
    depth = cache_k.shape[0]
    lanes = heads * 2 * tq
    flat = lambda c: c.reshape(depth, b, past * heads, vd)
    cache_spec = pl.BlockSpec((None, None, tkc * heads, vd), lambda bi, t: (layer, bi, t, 0))
    kern = functools.partial(_attn_cached_kernel, tq=tq, heads=heads, lam_init=lam_init)
    return pl.pallas_call(
        kern,
        grid=(b, past // tkc),
        in_specs=[
            pl.BlockSpec((4, ATT_DH), lambda bi, t: (0, 0)),
            pl.BlockSpec((1, ATT_VD), lambda bi, t: (0, 0)),
            pl.BlockSpec((1, tq, qkv.shape[-1]), lambda bi, t: (bi, 0, 0)),
            cache_spec, cache_spec,
        ],
        out_specs=pl.BlockSpec((1, tq, heads * ATT_VD), lambda bi, t: (bi, 0, 0)),
        out_shape=jax.ShapeDtypeStruct((b, tq, heads * ATT_VD), F32),
        scratch_shapes=[pltpu.VMEM((128, lanes), BF16), pltpu.VMEM((1, lanes), F32),
                        pltpu.VMEM((1, lanes), F32), pltpu.VMEM((ATT_VD, lanes), F32)],
        compiler_params=_params("parallel", "arbitrary"),
        name="diff_attn_cached",
    )(lam_p, g.reshape(1, ATT_VD), qkv, flat(cache_k), flat(cache_v))


def _attn_long_kernel(lam_ref, g_ref, q_ref, k_ref, vt_ref, kp_ref, vtp_ref, o_ref,
                      sa_ref, sb_ref, ta_ref, tb_ref, m_ref, acc_ref, *, tq, tk, n_pre_valid, lam_init):
    qi = pl.program_id(2)
    nt = (((1,), (1,)), ((), ()))

    q = q_ref[0]
    r = lax.broadcasted_iota(jnp.int32, (128, 128), 0)
    c = lax.broadcasted_iota(jnp.int32, (128, 128), 1)
    e1 = jnp.where((r == c) & (r < ATT_DH), 1.0, 0.0).astype(BF16)
    e2 = jnp.where((r == c) & (r >= ATT_DH), 1.0, 0.0).astype(BF16)
    q2t = jnp.concatenate([lax.dot_general(e1, q, nt, preferred_element_type=F32),
                           lax.dot_general(e2, q, nt, preferred_element_type=F32)], axis=1).astype(BF16)

    m_ref[...] = jnp.full(m_ref.shape, -jnp.inf, F32)
    acc_ref[...] = jnp.zeros(acc_ref.shape, F32)

    def scores(kt, mask=None):
        s = jnp.dot(kt, q2t, preferred_element_type=F32)
        return s if mask is None else jnp.where(mask, s, -jnp.inf)

    def scores_to(s_ref, top_ref, t, mask=None):
        off = pl.multiple_of(t * tk, tk)
        s = scores(k_ref[0, pl.ds(off, tk), :], mask)
        s_ref[...] = s
        top_ref[...] = jnp.max(s, axis=0, keepdims=True)

    def absorb(s, top, vt):
        m_old = m_ref[...]
        m_new = jnp.maximum(m_old, top)
        alpha = jnp.exp2(m_old - m_new)
        p = jnp.exp2(s - m_new).astype(BF16)
        m_ref[...] = m_new
        acc_ref[...] = alpha * acc_ref[...] + jnp.dot(vt, p, preferred_element_type=F32)

    def step(cur, nxt, t_next, t, mask=None):
        scores_to(nxt[0], nxt[1], t_next, mask)
        absorb(cur[0][...], cur[1][...], vt_ref[0, 0, t])

    buf_a = (sa_ref, ta_ref)
    buf_b = (sb_ref, tb_ref)

    n_pre = kp_ref.shape[1]
    key = lax.broadcasted_iota(jnp.int32, (n_pre, 2 * tq), 0)
    s_pre = scores(kp_ref[0], key < n_pre_valid)
    absorb(s_pre, jnp.max(s_pre, axis=0, keepdims=True), vtp_ref[0, 0])

    key = lax.broadcasted_iota(jnp.int32, (tk, 2 * tq), 0)
    qry = lax.broadcasted_iota(jnp.int32, (tk, 2 * tq), 1)
    qry = jnp.where(qry >= tq, qry - tq, qry)
    shift = CHUNK.bit_length() - 1
    n_full = 2 * qi
    last = jnp.maximum(n_full - 1, 0)
    scores_to(sa_ref, ta_ref, n_full, (key >> shift) <= (qry >> shift))
    step(buf_a, buf_b, n_full + 1, n_full, ((key + tk) >> shift) <= (qry >> shift))
    step(buf_b, buf_a, 0, n_full + 1)

    def pair(u, carry):
        step(buf_a, buf_b, 2 * u + 1, 2 * u)
        step(buf_b, buf_a, jnp.minimum(2 * u + 2, last), 2 * u + 1)
        return carry
    lax.fori_loop(0, qi, pair, 0)

    lq = lam_ref[...]
    lam = (jnp.exp(jnp.sum(lq[0:1] * lq[1:2], axis=-1, keepdims=True))
           - jnp.exp(jnp.sum(lq[2:3] * lq[3:4], axis=-1, keepdims=True)) + lam_init)
    acc = acc_ref[...]
    num = acc[:ATT_VD]
    den = acc[ATT_VD:ATT_VD + 1]
    o_t = num[:, :tq] / den[:, :tq] - lam * (num[:, tq:] / den[:, tq:])
    o_ref[0] = _rms(o_t.T, g_ref[...]) * (1.0 - lam_init)


VT_ROWS = ATT_VD + 16


def _values_t(v):
    ones = jnp.ones(v.shape[:-1] + (1,), v.dtype)
    zeros = jnp.zeros(v.shape[:-1] + (VT_ROWS - ATT_VD - 1,), v.dtype)
    return jnp.swapaxes(jnp.concatenate([v, ones, zeros], axis=-1), -1, -2)


def _attention_long(qkv, pre_k, pre_vt, n_pre_valid, lam_p, g, lam_init, tq, heads):
    b, L, _ = qkv.shape
    tk = tq // 2
    nt = L // tk
    lp = pre_k.shape[1]
    v = qkv[:, :, 2 * heads * 128:].reshape(b, nt, tk, heads, ATT_VD)
    vt = _values_t(v.transpose(0, 3, 1, 2, 4))
    kern = functools.partial(_attn_long_kernel, tq=tq, tk=tk, n_pre_valid=n_pre_valid, lam_init=lam_init)
    return pl.pallas_call(
        kern,
        grid=(b, heads, L // tq),
        in_specs=[
            pl.BlockSpec((4, ATT_DH), lambda bi, h, qi: (0, 0)),
            pl.BlockSpec((1, ATT_VD), lambda bi, h, qi: (0, 0)),
            pl.BlockSpec((1, tq, 128), lambda bi, h, qi: (bi, qi, h)),
            pl.BlockSpec((1, L, 128), lambda bi, h, qi: (bi, 0, heads + h)),
            pl.BlockSpec((1, 1, nt, VT_ROWS, tk), lambda bi, h, qi: (bi, h, 0, 0, 0)),
            pl.BlockSpec((1, lp, 128), lambda bi, h, qi: (bi, 0, h)),
            pl.BlockSpec((1, 1, VT_ROWS, lp), lambda bi, h, qi: (bi, h, 0, 0)),
        ],
        out_specs=pl.BlockSpec((1, tq, 128), lambda bi, h, qi: (bi, qi, h)),
        out_shape=jax.ShapeDtypeStruct((b, L, heads * ATT_VD), F32),
        scratch_shapes=[pltpu.VMEM((tk, 2 * tq), F32), pltpu.VMEM((tk, 2 * tq), F32),
                        pltpu.VMEM((1, 2 * tq), F32), pltpu.VMEM((1, 2 * tq), F32),
                        pltpu.VMEM((1, 2 * tq), F32), pltpu.VMEM((VT_ROWS, 2 * tq), F32)],
        compiler_params=_params("parallel", "parallel", "arbitrary"),
        name="diff_attn_long",
    )(lam_p, g.reshape(1, ATT_VD), qkv, qkv, vt, pre_k, pre_vt)


def _lru_kernel(p_ref, c0_ref, h0_ref, cw_ref, cb_ref, wa_ref, ba_ref, wx_ref, bx_ref, ll_ref,
                o_ref, hout_ref, cout_ref, ext_ref, hc_ref, *, tl, width, conv):
    li = pl.program_id(1)
    pad = 8

    @pl.when(li == 0)
    def _():
        ext_ref[pad - (conv - 1):pad] = c0_ref[0]
        hc_ref[...] = h0_ref[0]

    lx = p_ref[0, :, 0:width]
    lg = p_ref[0, :, width:2 * width]
    ext_ref[pad:pad + tl] = lx
    cw = cw_ref[...]
    xc = cb_ref[...] + cw[conv - 1:conv] * lx
    for s in range(1, conv):
        xc = xc + cw[conv - 1 - s:conv - s] * ext_ref[pad - s:pad - s + tl]
    tail = ext_ref[pad + tl - (conv - 1):pad + tl]
    cout_ref[0] = tail
    ext_ref[pad - (conv - 1):pad] = tail

    xb = xc.astype(BF16)
    r = jax.nn.sigmoid(jnp.dot(xb, wa_ref[...], preferred_element_type=F32) + ba_ref[...])
    i = jax.nn.sigmoid(jnp.dot(xb, wx_ref[...], preferred_element_type=F32) + bx_ref[...])
    z = -ll_ref[...]
    softplus = jnp.maximum(z, 0.0) + jnp.log1p(jnp.exp(-jnp.abs(z)))
    log_a = -LRU_C * r * softplus
    a = jnp.exp(log_a)
    u = jnp.sqrt(-jnp.tanh(log_a) * (1.0 + a * a)) * i * xc

    row = lax.broadcasted_iota(jnp.int32, a.shape, 0)
    d = 1
    while d < tl:
        keep = row >= d
        a_prev = jnp.where(keep, pltpu.roll(a, d, 0), 1.0)
        u_prev = jnp.where(keep, pltpu.roll(u, d, 0), 0.0)
        u = u + a * u_prev
        a = a * a_prev
        d *= 2
    h = u + a * hc_ref[...]
    hc_ref[...] = h[tl - 1:tl]
    hout_ref[0] = h[tl - 1:tl]
    o_ref[0] = h * jax.nn.gelu(lg)


def _lru(proj, c0, h0, cw, cb, wa, ba, wx, bx, ll, tl):
    b, L, _ = proj.shape
    conv, width = cw.shape
    vec = lambda a: a.reshape(1, width)
    const = lambda shape: pl.BlockSpec(shape, lambda bi, li: (0,) * len(shape))
    kern = functools.partial(_lru_kernel, tl=tl, width=width, conv=conv)
    return pl.pallas_call(
        kern,
        grid=(b, L // tl),
        in_specs=[
            pl.BlockSpec((1, tl, 2 * width), lambda bi, li: (bi, li, COL_LRU)),
            pl.BlockSpec((1, conv - 1, width), lambda bi, li: (bi, 0, 0)),
            pl.BlockSpec((1, 1, width), lambda bi, li: (bi, 0, 0)),
            const((conv, width)), const((1, width)),
            const((width, width)), const((1, width)),
            const((width, width)), const((1, width)), const((1, width)),
        ],
        out_specs=[
            pl.BlockSpec((1, tl, width), lambda bi, li: (bi, li, 0)),
            pl.BlockSpec((1, 1, width), lambda bi, li: (bi, 0, 0)),
            pl.BlockSpec((1, conv - 1, width), lambda bi, li: (bi, 0, 0)),
        ],
        out_shape=[jax.ShapeDtypeStruct((b, L, width), F32),
                   jax.ShapeDtypeStruct((b, 1, width), F32),
                   jax.ShapeDtypeStruct((b, conv - 1, width), F32)],
        scratch_shapes=[pltpu.VMEM((tl + 8, width), F32), pltpu.VMEM((1, width), F32)],
        compiler_params=_params("parallel", "arbitrary"),
        name="rg_lru",
    )(proj, c0, h0.reshape(b, 1, width), cw, vec(cb), wa, vec(ba), wx, vec(bx), vec(ll))


def _gla_kernel(pa_ref, pb_ref, s0_ref, w2_ref, gb_ref, gn_ref, o_ref, sout_ref, st_ref, oi_ref,
                *, tl, chunk, kw, vw):
    li = pl.program_id(1)

    @pl.when(li == 0)
    def _():
        st_ref[...] = s0_ref[0]

    q = pa_ref[0, :, 0:kw] * (GLA_DK ** -0.5)
    k = pa_ref[0, :, kw:2 * kw]
    v = pa_ref[0, :, 2 * kw:2 * kw + vw]
    gog = pb_ref[0, :, 0:vw]
    glr = pb_ref[0, :, vw:2 * vw]

    x = jnp.dot(glr.astype(BF16), w2_ref[...], preferred_element_type=F32) + gb_ref[...]
    log_g = (jnp.minimum(x, 0.0) - jnp.log1p(jnp.exp(-jnp.abs(x)))) / GLA_TAU

    rowc = lax.broadcasted_iota(jnp.int32, (tl, kw), 0) % chunk
    bc = log_g
    d = 1
    while d < chunk:
        bc = bc + jnp.where(rowc >= d, pltpu.roll(bc, d, 0), 0.0)
        d *= 2

    hk = lax.broadcasted_iota(jnp.int32, (kw, vw), 0) // GLA_DK
    hv = lax.broadcasted_iota(jnp.int32, (kw, vw), 1) // GLA_DV
    spread = jnp.where(hk == hv, 1.0, 0.0).astype(BF16)

    sub = min(GLA_SUB, chunk)
    row_s = rowc % sub
    o_band = jnp.dot((q * k).astype(BF16), spread, preferred_element_type=F32) * v
    for dl in range(1, sub):
        valid = row_s >= dl
        rel = jnp.where(valid, bc - pltpu.roll(bc, dl, 0), -jnp.inf)
        w = q * pltpu.roll(k, dl, 0) * jnp.exp(rel)
        att = jnp.dot(w.astype(BF16), spread, preferred_element_type=F32)
        o_band = o_band + att * pltpu.roll(v, dl, 0)
    oi_ref[...] = o_band

    n_sub = chunk // sub
    nt_dims = (((1,), (1,)), ((), ()))
    hs = lax.broadcasted_iota(jnp.int32, (vw, kw), 0) // GLA_DV
    hl = lax.broadcasted_iota(jnp.int32, (vw, kw), 1) // GLA_DK
    same_head = hs == hl
    heads = kw // GLA_DK
    if n_sub > 1:
        cat = (n_sub - 1) * kw
        kr = lax.broadcasted_iota(jnp.int32, (heads * chunk, cat), 0) // chunk
        kl = (lax.broadcasted_iota(jnp.int32, (heads * chunk, cat), 1) % kw) // GLA_DK
        key_head = kr == kl
        vr = lax.broadcasted_iota(jnp.int32, (heads * chunk, vw), 0) // chunk
        vl = lax.broadcasted_iota(jnp.int32, (heads * chunk, vw), 1) // GLA_DV
        val_head = vr == vl
        ri = lax.broadcasted_iota(jnp.int32, (chunk, kw), 0)
    for c in range(tl // chunk):
        rows = slice(c * chunk, (c + 1) * chunk)
        b_c = bc[rows]
        q_c = q[rows]
        k_c = k[rows]
        v_c = v[rows].astype(BF16)
        o_c = jnp.zeros((chunk, vw), F32)
        if n_sub > 1:
            qs, ks = [], []
            for i in range(1, n_sub):
                r_i = b_c[i * sub - 1:i * sub]
                in_q = (ri >= i * sub) & (ri < (i + 1) * sub)
                in_k = ri < i * sub
                qs.append(jnp.where(in_q, q_c * jnp.exp(jnp.where(in_q, b_c - r_i, 0.0)), 0.0))
                ks.append(jnp.where(in_k, k_c * jnp.exp(jnp.where(in_k, r_i - b_c, 0.0)), 0.0))
            q_cat = jnp.concatenate(qs, axis=1).astype(BF16)
            k_cat = jnp.concatenate(ks, axis=1).astype(BF16)
            k_bd = jnp.where(key_head, jnp.concatenate([k_cat] * heads, axis=0), jnp.zeros_like(k_cat[:1]))
            att = lax.dot_general(q_cat, k_bd, nt_dims, preferred_element_type=F32)
            v_bd = jnp.where(val_head, jnp.concatenate([v_c] * heads, axis=0), jnp.zeros_like(v_c[:1]))
            o_c = jnp.dot(att.astype(BF16), v_bd, preferred_element_type=F32)
        b_last = b_c[chunk - 1:chunk]
        st = st_ref[...]
        qe = (q_c * jnp.exp(b_c)).astype(BF16)
        o_c = o_c + lax.dot_general(qe, st.astype(BF16), nt_dims, preferred_element_type=F32)
        ke = (k_c * jnp.exp(b_last - b_c)).astype(BF16)
        kv_t = lax.dot_general(v_c, ke, (((0,), (0,)), ((), ())), preferred_element_type=F32)
        st_ref[...] = jnp.exp(b_last) * st + jnp.where(same_head, kv_t, 0.0)
        oi_ref[rows, :] += o_c
    sout_ref[0] = st_ref[...]

    o = oi_ref[...]
    o2 = o * o
    lane_h = lax.broadcasted_iota(jnp.int32, o.shape, 1) // GLA_DV
    ms = jnp.zeros_like(o)
    for h in range(vw // GLA_DV):
        in_h = lane_h == h
        s_h = jnp.sum(jnp.where(in_h, o2, 0.0), axis=-1, keepdims=True) * (1.0 / GLA_DV)
        ms = jnp.where(in_h, s_h, ms)
    o_ref[0] = o * lax.rsqrt(ms + EPS) * gn_ref[...] * jax.nn.silu(gog)


def _gla(proj, s0_t, w2p, gb, gn_t, tl):
    b, L, _ = proj.shape
    vw, kw = s0_t.shape[1:]
    chunk = min(CHUNK, L)
    const = lambda shape: pl.BlockSpec(shape, lambda bi, li: (0,) * len(shape))
    kern = functools.partial(_gla_kernel, tl=tl, chunk=chunk, kw=kw, vw=vw)
    return pl.pallas_call(
        kern,
        grid=(b, L // tl),
        in_specs=[
            pl.BlockSpec((1, tl, PROJ_TN), lambda bi, li: (bi, li, COL_GLA_A)),
            pl.BlockSpec((1, tl, PROJ_TN), lambda bi, li: (bi, li, COL_GLA_B)),
            pl.BlockSpec((1, vw, kw), lambda bi, li: (bi, 0, 0)),
            const((vw, kw)), const((1, kw)), const((1, vw)),
        ],
        out_specs=[
            pl.BlockSpec((1, tl, vw), lambda bi, li: (bi, li, 0)),
            pl.BlockSpec((1, vw, kw), lambda bi, li: (bi, 0, 0)),
        ],
        out_shape=[jax.ShapeDtypeStruct((b, L, vw), F32), jax.ShapeDtypeStruct((b, vw, kw), F32)],
        scratch_shapes=[pltpu.VMEM((vw, kw), F32), pltpu.VMEM((tl, vw), F32)],
        compiler_params=_params("parallel", "arbitrary"),
        name="gla",
    )(proj, proj, s0_t, w2p, gb.reshape(1, kw), gn_t.reshape(1, vw))


def _ffn_kernel(x_ref, oa_ref, ol_ref, og_ref, wo_ref, g_ref, c0_ref, wu_ref, wg_ref, cw_ref, cb_ref,
                wd_ref, gf_ref, out_ref, cout_ref, acc_ref, hn_ref, carry_ref,
                *, bb, tl, tn, conv, nsplit, final_norm):
    li = pl.program_id(1)
    j = pl.program_id(2)
    m = bb * tl
    d = x_ref.shape[-1]
    wa = oa_ref.shape[-1]
    wl = ol_ref.shape[-1]
    acc = out_ref.at[0] if bb == 1 else acc_ref

    @pl.when(j == 0)
    def _():
        mix = jnp.dot(oa_ref[...].reshape(m, wa).astype(BF16), wo_ref[0:wa], preferred_element_type=F32)
        mix += jnp.dot(ol_ref[...].reshape(m, wl).astype(BF16), wo_ref[wa:wa + wl],
                       preferred_element_type=F32)
        mix += jnp.dot(og_ref[...].reshape(m, d - wa - wl).astype(BF16), wo_ref[wa + wl:d],
                       preferred_element_type=F32)
        xn = x_ref[...].reshape(m, d) + mix
        acc[...] = xn
        hn_ref[...] = _rms(xn, g_ref[...]).astype(BF16)

    @pl.when(li == 0)
    def _():
        carry_ref[j] = c0_ref[...]

    prev = carry_ref[j]
    ts = tl // nsplit
    t = lax.broadcasted_iota(jnp.int32, (bb, ts, tn), 1)
    cw = cw_ref[...]
    def up_gate(p):
        hn = hn_ref[p * ts * bb:(p + 1) * ts * bb]
        return (jnp.dot(hn, wu_ref[...], preferred_element_type=F32),
                jnp.dot(hn, wg_ref[...], preferred_element_type=F32))

    ahead = up_gate(0)
    for p in range(nsplit):
        rows = slice(p * ts * bb, (p + 1) * ts * bb)
        u, gate = ahead
        if p + 1 < nsplit:
            ahead = up_gate(p + 1)
        u3 = u.reshape(bb, ts, tn)
        uc = cb_ref[...] + cw[conv - 1:conv] * u3
        for s in range(1, conv):
            shifted = pltpu.roll(u, s, 0).reshape(bb, ts, tn)
            for r in range(s):
                shifted = jnp.where(t == r, prev[:, conv - 1 - s + r:conv - s + r, :], shifted)
            uc = uc + cw[conv - 1 - s:conv - s] * shifted
        prev = u3[:, ts - (conv - 1):ts, :]
        act = (jax.nn.gelu(uc) * gate.reshape(bb, ts, tn)).reshape(bb * ts, tn).astype(BF16)
        acc[rows] += jnp.dot(act, wd_ref[...], preferred_element_type=F32)
    carry_ref[j] = prev
    cout_ref[:, j] = prev

    @pl.when(j == pl.num_programs(2) - 1)
    def _():
        if bb == 1:
            if final_norm:
                acc[...] = _rms(acc[...], gf_ref[...])
        else:
            y = acc_ref[...]
            if final_norm:
                y = _rms(y, gf_ref[...])
            out_ref[...] = y.reshape(bb, tl, d)


def _ffn(x, oa, ol, og, wo, g, c0, wu, wg, cw, cb, wd, gf, bb, tl, tn, final_norm):
    b, L, d = x.shape
    conv, f = cw.shape
    nj = f // tn
    nsplit = tl // FFN_PIECE if bb == 1 and tl % FFN_PIECE == 0 else 1
    kern = functools.partial(_ffn_kernel, bb=bb, tl=tl, tn=tn, conv=conv, nsplit=nsplit,
                             final_norm=final_norm)
    act_spec = lambda w: pl.BlockSpec((bb, tl, w), lambda bi, li, j: (bi, li, 0))
    const = lambda shape: pl.BlockSpec(shape, lambda bi, li, j: (0,) * len(shape))
    return pl.pallas_call(
        kern,
        grid=(b // bb, L // tl, nj),
        in_specs=[
            act_spec(d), act_spec(oa.shape[-1]), act_spec(ol.shape[-1]), act_spec(og.shape[-1]),
            const((d, d)), const((1, d)),
            pl.BlockSpec((bb, conv - 1, tn), lambda bi, li, j: (bi, 0, j)),
            pl.BlockSpec((d, tn), lambda bi, li, j: (0, j)),
            pl.BlockSpec((d, tn), lambda bi, li, j: (0, j)),
            pl.BlockSpec((conv, tn), lambda bi, li, j: (0, j)),
            pl.BlockSpec((1, tn), lambda bi, li, j: (0, j)),
            pl.BlockSpec((tn, d), lambda bi, li, j: (j, 0)),
            const((1, d)),
        ],
        out_specs=[
            pl.BlockSpec((bb, tl, d), lambda bi, li, j: (bi, li, 0)),
            pl.BlockSpec((bb, nj, conv - 1, tn), lambda bi, li, j: (bi, 0, 0, 0)),
        ],
        out_shape=[jax.ShapeDtypeStruct((b, L, d), F32), jax.ShapeDtypeStruct((b, nj, conv - 1, tn), F32)],
        scratch_shapes=[pltpu.VMEM((8, 128) if bb == 1 else (bb * tl, d), F32),
                        pltpu.VMEM((bb * tl, d), BF16), pltpu.VMEM((nj, bb, conv - 1, tn), F32)],
        compiler_params=pltpu.CompilerParams(dimension_semantics=("parallel", "arbitrary", "arbitrary"),
                                             vmem_limit_bytes=FFN_VMEM_LIMIT),
        name="mix_ffn",
    )(x, oa, ol, og, wo, g.reshape(1, d), c0, wu, wg, cw, cb.reshape(1, f), wd, gf.reshape(1, d))


def _block_diag(w):
    n, c, dd = w.shape
    eye = jnp.eye(n, dtype=w.dtype)
    return jnp.einsum('ncd,nm->ncmd', w, eye).reshape(n * c, n * dd)


def _prep_layer(l, w_in, lru_gate_a_w, lru_gate_x_w, gla_gate_w2, gla_norm_g, w_out, ffn_w_up,
                ffn_w_gate, ffn_w_down, sizes):
    att_w, lru_w, gla_kw, gla_vw = sizes
    o = 0
    parts = {}
    for name, wdt in (("q", att_w), ("k", att_w), ("v", att_w), ("lx", lru_w), ("lg", lru_w),
                      ("gq", gla_kw), ("gk", gla_kw), ("gv", gla_vw), ("glr", GLA_RANK), ("gog", gla_vw)):
        parts[name] = w_in[l][:, o:o + wdt]
        o += wdt
    d = w_in.shape[1]
    pad = jnp.zeros((d, PROJ_W - o), w_in.dtype)
    order = ("q", "k", "v", "lx", "lg", "gq", "gk", "gv", "gog", "glr")
    w_in_p = jnp.concatenate([parts[n] for n in order] + [pad], axis=1).astype(BF16)
    w2p = jnp.zeros((gla_vw, gla_kw), F32).at[:GLA_RANK].set(gla_gate_w2[l]).astype(BF16)
    return dict(
        w_in=w_in_p,
        wa=_block_diag(lru_gate_a_w[l]).astype(BF16),
        wx=_block_diag(lru_gate_x_w[l]).astype(BF16),
        w2p=w2p,
        gn_t=jnp.tile(gla_norm_g[l], gla_vw // GLA_DV),
        w_out=w_out[l].astype(BF16),
        wu=ffn_w_up[l].astype(BF16), wg=ffn_w_gate[l].astype(BF16), wd=ffn_w_down[l].astype(BF16),
    )


def _state_to_bd(s):
    b, h, dk, dv = s.shape
    eye = jnp.eye(h, dtype=s.dtype)
    return jnp.einsum('bhde,hg->bhegd', s, eye).reshape(b, h * dv, h * dk)


def _state_from_bd(st, h):
    b, vw, kw = st.shape
    dv, dk = vw // h, kw // h
    s5 = st.reshape(b, h, dv, h, dk)
    diag = jnp.stack([s5[:, i, :, i, :] for i in range(h)], axis=1)
    return diag.transpose(0, 1, 3, 2)


def _pick_tile(n, target):
    t = min(n, target)
    while n % t:
        t //= 2
    return t


def _trunk(x, states, pre_kv, n_pre_valid, layers, prm, heads, gla_heads, tiles, long_attn=False,
           cache=None):
    b, L, d = x.shape
    depth = len(layers)
    tm, tq, tl_lru, tl_gla, bb, tl_ffn, tn = tiles
    outs = []
    for l in range(depth):
        lw = layers[l]
        width = prm['lru_conv_w'].shape[-1]
        f = prm['ffn_conv_w'].shape[-1]
        if states is None:
            h0 = jnp.zeros((b, width), F32)
            c0 = jnp.zeros((b, prm['lru_conv_w'].shape[1] - 1, width), F32)
            s0 = jnp.zeros((b, gla_heads, GLA_DK, GLA_DV), F32)
            fc0 = jnp.zeros((b, prm['ffn_conv_w'].shape[1] - 1, f), F32)
        else:
            h0, c0, s0, fc0 = states[l]
        k32, v32, pf, pb = _in_proj(x.reshape(b * L, d), prm['norm_mix_g'][l], lw['w_in'], tm)
        pf = pf.reshape(b, L, MIX_W)
        pb = pb.reshape(b, L, QKV_W)
        lam_init = 0.8 - 0.6 * math.exp(-0.3 * l)
        lam_p, sub_g = prm['attn_lambda'][l], prm['attn_subln_g'][l]
        if cache is not None:
            o_att = _attention_cached(pb, cache[0], cache[1], l, lam_p, sub_g, lam_init,
                                      _pick_tile(cache[0].shape[2], 1024))
        elif long_attn:
            o_att = _attention_long(pb, pre_kv[l][0], pre_kv[l][1], n_pre_valid, lam_p, sub_g, lam_init,
                                    tq, heads)
        else:
            o_att = _attention(pb, None, None, 0, lam_p, sub_g, lam_init, tq, heads)
        o_lru, h1, c1 = _lru(pf, c0, h0, prm['lru_conv_w'][l], prm['lru_conv_b'][l], lw['wa'],
                             prm['lru_gate_a_b'][l], lw['wx'], prm['lru_gate_x_b'][l],
                             prm['lru_log_lambda'][l], tl_lru)
        o_gla, s1_t = _gla(pf, _state_to_bd(s0), lw['w2p'], prm['gla_gate_b'][l], lw['gn_t'], tl_gla)
        x, fc1 = _ffn(x, o_att, o_lru, o_gla, lw['w_out'], prm['norm_ffn_g'][l], fc0, lw['wu'], lw['wg'],
                      prm['ffn_conv_w'][l], prm['ffn_conv_b'][l], lw['wd'], prm['norm_final_g'],
                      bb, tl_ffn, tn, final_norm=(l == depth - 1))
        fc1 = fc1.transpose(0, 2, 1, 3).reshape(b, fc1.shape[2], f)
        aw = heads * ATT_VD
        outs.append(dict(k=k32.reshape(b, L, heads, ATT_VD), v=v32.reshape(b, L, heads, ATT_VD),
                         kb=pb[:, :, aw:2 * aw],
                         vb=pb[:, :, 2 * aw:3 * aw], h=h1.reshape(b, width), c=c1,
                         s=_state_from_bd(s1_t, gla_heads), fc=fc1))
    return x, outs


def kernel(x_prompt, x_sample, cache_attn_k, cache_attn_v, state_lru_h, state_lru_conv, state_gla, state_ffn_conv, meta_tokens, norm_mix_g, w_in, attn_lambda, attn_subln_g, lru_conv_w, lru_conv_b, lru_gate_a_w, lru_gate_a_b, lru_gate_x_w, lru_gate_x_b, lru_log_lambda, gla_gate_w2, gla_gate_b, gla_norm_g, w_out, norm_ffn_g, ffn_w_up, ffn_conv_w, ffn_conv_b, ffn_w_gate, ffn_w_down, norm_final_g):
    depth = w_in.shape[0]
    b, seq, d = x_prompt.shape
    db, dseq, _ = x_sample.shape
    past = cache_attn_k.shape[2]
    heads = cache_attn_k.shape[3]
    gla_heads = state_gla.shape[2]
    n_meta = meta_tokens.shape[0]
    width = lru_conv_w.shape[-1]
    f = ffn_conv_w.shape[-1]
    att_w = heads * ATT_VD
    sizes = (att_w, width, gla_heads * GLA_DK, gla_heads * GLA_DV)

    prm = dict(norm_mix_g=norm_mix_g, attn_lambda=attn_lambda, attn_subln_g=attn_subln_g,
               lru_conv_w=lru_conv_w, lru_conv_b=lru_conv_b, lru_gate_a_b=lru_gate_a_b,
               lru_gate_x_b=lru_gate_x_b, lru_log_lambda=lru_log_lambda, gla_gate_b=gla_gate_b,
               norm_ffn_g=norm_ffn_g, ffn_conv_w=ffn_conv_w, ffn_conv_b=ffn_conv_b,
               norm_final_g=norm_final_g)
    layers = [_prep_layer(l, w_in, lru_gate_a_w, lru_gate_x_w, gla_gate_w2, gla_norm_g, w_out,
                          ffn_w_up, ffn_w_gate, ffn_w_down, sizes) for l in range(depth)]
    tn = 256 if f % 256 == 0 else 128

    xm = jnp.broadcast_to(meta_tokens.astype(F32)[None], (b, n_meta, d))
    tiles_m = (b * n_meta, n_meta, n_meta, n_meta, b, n_meta, tn)
    _, om = _trunk(xm, None, None, 0, layers, prm, heads, gla_heads, tiles_m)

    pad_rows = 128 - n_meta
    pre_f = [(jnp.pad(o['kb'], ((0, 0), (0, pad_rows), (0, 0))),
              _values_t(jnp.pad(o['vb'], ((0, 0), (0, pad_rows), (0, 0)))
                        .reshape(b, 128, heads, ATT_VD).transpose(0, 2, 1, 3))) for o in om]
    st_f = [(o['h'], o['c'], o['s'], o['fc']) for o in om]
    tiles_f = (_pick_tile(b * seq, 1024), _pick_tile(seq, 1024), _pick_tile(seq, 512),
               _pick_tile(seq, 256), 1, _pick_tile(seq, 1024), f // 2)
    yp, of = _trunk(x_prompt, st_f, pre_f, n_meta, layers, prm, heads, gla_heads, tiles_f,
                    long_attn=True)

    st_s = [(state_lru_h[l], state_lru_conv[l], state_gla[l], state_ffn_conv[l]) for l in range(depth)]
    tiles_s = (_pick_tile(db * dseq, 1024), dseq, dseq, dseq, db, dseq, tn)
    ys, os_ = _trunk(x_sample, st_s, None, past, layers, prm, heads, gla_heads, tiles_s,
                     cache=(cache_attn_k, cache_attn_v))

    def stack_p(name, tail_shape):
        return jnp.stack([jnp.concatenate([m[name].reshape((b, n_meta) + tail_shape),
                                           fr[name].reshape((b, seq) + tail_shape)], axis=1)
                          for m, fr in zip(om, of)])

    k_p = stack_p('k', (heads, ATT_VD))
    v_p = stack_p('v', (heads, ATT_VD))
    k_s = jnp.stack([o['k'].reshape(db, dseq, heads, ATT_VD) for o in os_])
    v_s = jnp.stack([o['v'].reshape(db, dseq, heads, ATT_VD) for o in os_])
    st = lambda outs, name: jnp.stack([o[name] for o in outs])
    return (yp, ys, k_p, v_p, st(of, 'h'), st(of, 'c'), st(of, 's'), st(of, 'fc'),
            k_s, v_s, st(os_, 'h'), st(os_, 'c'), st(os_, 's'), st(os_, 'fc'))
```

```python
import functools
import math

import jax
import jax.numpy as jnp
from jax import lax
from jax.experimental import pallas as pl
from jax.experimental.pallas import tpu as pltpu

F32 = jnp.float32
BF16 = jnp.bfloat16

CHUNK = 64
EPS = 1e-6
ATT_DH = 64
ATT_VD = 128
LRU_C = 8.0
GLA_DK = 32
GLA_DV = 64
GLA_RANK = 16
GLA_TAU = 16.0
GLA_SUB = 16
FFN_PIECE = 256

PROJ_TN = 512
PROJ_W = 3072
QKV_W = 1536
MIX_W = PROJ_W - QKV_W
COL_LRU = 0
COL_GLA_A = 1
COL_GLA_B = 2

VMEM_LIMIT = 48 * 1024 * 1024
FFN_VMEM_LIMIT = 58 * 1024 * 1024


def _rms(x, g):
    return x * lax.rsqrt(jnp.mean(x * x, axis=-1, keepdims=True) + EPS) * g


def _params(*sem):
    return pltpu.CompilerParams(dimension_semantics=sem, vmem_limit_bytes=VMEM_LIMIT)


def _in_proj_kernel(x_ref, g_ref, w_ref, k_ref, v_ref, mix_ref, ob_ref, hn_ref, *, n_bf_tiles, q_scale):
    j = pl.program_id(1)

    @pl.when(j == 0)
    def _():
        hn_ref[...] = _rms(x_ref[...], g_ref[...]).astype(BF16)

    y = jnp.dot(hn_ref[...], w_ref[...], preferred_element_type=F32)

    @pl.when(j == 1)
    def _():
        for h in range(k_ref.shape[1]):
            k_ref[:, h, :] = y[:, h * 128:(h + 1) * 128]

    @pl.when(j == 2)
    def _():
        for h in range(v_ref.shape[1]):
            v_ref[:, h, :] = y[:, h * 128:(h + 1) * 128]

    @pl.when(j >= n_bf_tiles)
    def _():
        mix_ref[...] = y

    @pl.when(j < n_bf_tiles)
    def _():
        scale = jnp.where(j == 0, q_scale, 1.0).astype(F32)
        ob_ref[...] = (y * scale).astype(BF16)


def _in_proj(x2, g, w, tm):
    m, d = x2.shape
    n_bf = QKV_W // PROJ_TN
    kv_heads = PROJ_TN // 128
    kern = functools.partial(_in_proj_kernel, n_bf_tiles=n_bf, q_scale=ATT_DH ** -0.5 * math.log2(math.e))
    return pl.pallas_call(
        kern,
        grid=(m // tm, PROJ_W // PROJ_TN),
        in_specs=[
            pl.BlockSpec((tm, d), lambda i, j: (i, 0)),
            pl.BlockSpec((1, d), lambda i, j: (0, 0)),
            pl.BlockSpec((d, PROJ_TN), lambda i, j: (0, j)),
        ],
        out_specs=[
            pl.BlockSpec((tm, kv_heads, 128), lambda i, j: (i, 0, 0)),
            pl.BlockSpec((tm, kv_heads, 128), lambda i, j: (i, 0, 0)),
            pl.BlockSpec((tm, PROJ_TN), lambda i, j: (i, jnp.maximum(j - n_bf, 0))),
            pl.BlockSpec((tm, PROJ_TN), lambda i, j: (i, jnp.minimum(j, n_bf - 1))),
        ],
        out_shape=[jax.ShapeDtypeStruct((m, kv_heads, 128), F32), jax.ShapeDtypeStruct((m, kv_heads, 128), F32),
                   jax.ShapeDtypeStruct((m, MIX_W), F32), jax.ShapeDtypeStruct((m, QKV_W), BF16)],
        scratch_shapes=[pltpu.VMEM((tm, d), BF16)],
        compiler_params=_params("parallel", "arbitrary"),
        name="in_proj",
    )(x2, g.reshape(1, d), w)


def _attn_kernel(*refs, tq, tk, n_pre, n_pre_valid, tkp, lam_init, has_pre):
    if has_pre:
        lam_ref, g_ref, q_ref, k_ref, v_ref, kp_ref, vp_ref, o_ref, m_ref, l_ref, acc_ref = refs
    else:
        lam_ref, g_ref, q_ref, k_ref, v_ref, o_ref, m_ref, l_ref, acc_ref = refs
    qi = pl.program_id(2)

    q = q_ref[0]
    lane = lax.broadcasted_iota(jnp.int32, q.shape, 1)
    zero = jnp.zeros_like(q)
    q2 = jnp.concatenate([jnp.where(lane < ATT_DH, q, zero), jnp.where(lane >= ATT_DH, q, zero)], axis=0)

    m_ref[...] = jnp.full(m_ref.shape, -jnp.inf, F32)
    l_ref[...] = jnp.zeros(l_ref.shape, F32)
    acc_ref[...] = jnp.zeros(acc_ref.shape, F32)

    def update(kt, vt, mask):
        s = lax.dot_general(q2, kt.astype(BF16), (((1,), (1,)), ((), ())), preferred_element_type=F32)
        if mask is not None:
            s = jnp.where(mask, s, -jnp.inf)
        m_old = m_ref[...]
        m_new = jnp.maximum(m_old, jnp.max(s, axis=-1, keepdims=True))
        alpha = jnp.exp2(m_old - m_new)
        p = jnp.exp2(s - m_new)
        l_ref[...] = alpha * l_ref[...] + jnp.sum(p, axis=-1, keepdims=True)
        m_ref[...] = m_new
        acc_ref[...] = alpha * acc_ref[...] + jnp.dot(p.astype(BF16), vt.astype(BF16),
                                                      preferred_element_type=F32)

    if has_pre:
        n_tiles = n_pre // tkp
        full_tiles = n_pre_valid // tkp
        if full_tiles > 0:
            def pre_body(t, c):
                off = pl.multiple_of(t * tkp, tkp)
                update(kp_ref[0, pl.ds(off, tkp), :], vp_ref[0, pl.ds(off, tkp), :], None)
                return c
            lax.fori_loop(0, full_tiles, pre_body, 0)
        if full_tiles < n_tiles:
            col = lax.broadcasted_iota(jnp.int32, (2 * tq, tkp), 1)
            off = full_tiles * tkp
            update(kp_ref[0, off:off + tkp, :], vp_ref[0, off:off + tkp, :],
                   col < (n_pre_valid - off))

    def own_body(t, c):
        off = pl.multiple_of(t * tk, tk)
        update(k_ref[0, pl.ds(off, tk), :], v_ref[0, pl.ds(off, tk), :], None)
        return c
    lax.fori_loop(0, qi * (tq // tk), own_body, 0)

    off = pl.multiple_of(qi * tq, tq)
    if tq > CHUNK:
        row = lax.broadcasted_iota(jnp.int32, (2 * tq, tq), 0)
        col = lax.broadcasted_iota(jnp.int32, (2 * tq, tq), 1)
        row = jnp.where(row >= tq, row - tq, row)
        mask = (col // CHUNK) <= (row // CHUNK)
    else:
        mask = None
    update(k_ref[0, pl.ds(off, tq), :], v_ref[0, pl.ds(off, tq), :], mask)

    lq = lam_ref[...]
    lam = (jnp.exp(jnp.sum(lq[0:1] * lq[1:2], axis=-1, keepdims=True))
           - jnp.exp(jnp.sum(lq[2:3] * lq[3:4], axis=-1, keepdims=True)) + lam_init)
    acc = acc_ref[...]
    l = l_ref[...]
    o = acc[:tq] / l[:tq] - lam * (acc[tq:] / l[tq:])
    o_ref[0] = _rms(o, g_ref[...]) * (1.0 - lam_init)


def _attention(qkv, pre_k, pre_v, n_pre_valid, lam_p, g, lam_init, tq, heads):
    b, L, _ = qkv.shape
    has_pre = pre_k is not None
    tk = tq
    kern_kw = dict(tq=tq, tk=tk, lam_init=lam_init, has_pre=has_pre, n_pre=0, n_pre_valid=0, tkp=0)
    in_specs = [
        pl.BlockSpec((4, ATT_DH), lambda bi, h, qi: (0, 0)),
        pl.BlockSpec((1, ATT_VD), lambda bi, h, qi: (0, 0)),
        pl.BlockSpec((1, tq, 128), lambda bi, h, qi: (bi, qi, h)),
        pl.BlockSpec((1, L, 128), lambda bi, h, qi: (bi, 0, heads + h)),
        pl.BlockSpec((1, L, 128), lambda bi, h, qi: (bi, 0, 2 * heads + h)),
    ]
    args = [lam_p, g.reshape(1, ATT_VD), qkv, qkv, qkv]
    if has_pre:
        lp = pre_k.shape[1]
        tkp = min(lp, 512)
        kern_kw.update(n_pre=lp, n_pre_valid=n_pre_valid, tkp=tkp)
        in_specs += [pl.BlockSpec((1, lp, 128), lambda bi, h, qi: (bi, 0, h)),
                     pl.BlockSpec((1, lp, 128), lambda bi, h, qi: (bi, 0, h))]
        args += [pre_k, pre_v]
    return pl.pallas_call(
        functools.partial(_attn_kernel, **kern_kw),
        grid=(b, heads, L // tq),
        in_specs=in_specs,
        out_specs=pl.BlockSpec((1, tq, 128), lambda bi, h, qi: (bi, qi, h)),
        out_shape=jax.ShapeDtypeStruct((b, L, heads * ATT_VD), F32),
        scratch_shapes=[pltpu.VMEM((2 * tq, 1), F32), pltpu.VMEM((2 * tq, 1), F32),
                        pltpu.VMEM((2 * tq, ATT_VD), F32)],
        compiler_params=_params("parallel", "parallel", "arbitrary"),
        name="diff_attn",
    )(*args)


def _attn_cached_kernel(lam_ref, g_ref, qkv_ref, kc_ref, vc_ref, o_ref,
                        q2t_ref, m_ref, l_ref, acc_ref, *, tq, heads, lam_init):
    t = pl.program_id(1)
    nt = (((1,), (1,)), ((), ()))
    tn = (((0,), (0,)), ((), ()))
    w = heads * 128
    lanes = heads * 2 * tq

    @pl.when(t == 0)
    def _():
        r = lax.broadcasted_iota(jnp.int32, (128, 128), 0)
        c = lax.broadcasted_iota(jnp.int32, (128, 128), 1)
        e1 = jnp.where((r == c) & (r < ATT_DH), 1.0, 0.0).astype(BF16)
        e2 = jnp.where((r == c) & (r >= ATT_DH), 1.0, 0.0).astype(BF16)
        parts = []
        for h in range(heads):
            q = qkv_ref[0, :, h * 128:(h + 1) * 128]
            parts += [lax.dot_general(e1, q, nt, preferred_element_type=F32),
                      lax.dot_general(e2, q, nt, preferred_element_type=F32)]
        q2t_ref[...] = jnp.concatenate(parts, axis=1).astype(BF16)
        m_ref[...] = jnp.full(m_ref.shape, -jnp.inf, F32)
        l_ref[...] = jnp.zeros(l_ref.shape, F32)
        acc_ref[...] = jnp.zeros(acc_ref.shape, F32)

    def absorb(kt, vt, row_head):
        s = jnp.dot(kt.astype(BF16), q2t_ref[...], preferred_element_type=F32)
        lane_head = lax.broadcasted_iota(jnp.int32, s.shape, 1) // (2 * tq)
        s = jnp.where(row_head == lane_head, s, -jnp.inf)
        m_old = m_ref[...]
        m_new = jnp.maximum(m_old, jnp.max(s, axis=0, keepdims=True))
        alpha = jnp.exp2(m_old - m_new)
        p = jnp.exp2(s - m_new)
        l_ref[...] = alpha * l_ref[...] + jnp.sum(p, axis=0, keepdims=True)
        m_ref[...] = m_new
        acc_ref[...] = alpha * acc_ref[...] + lax.dot_general(vt.astype(BF16), p.astype(BF16), tn,
                                                              preferred_element_type=F32)

    rows_c = kc_ref.shape[0]
    absorb(kc_ref[...], vc_ref[...], lax.broadcasted_iota(jnp.int32, (rows_c, lanes), 0) % heads)

    @pl.when(t == pl.num_programs(1) - 1)
    def _():
        lq = lam_ref[...]
        lam = (jnp.exp(jnp.sum(lq[0:1] * lq[1:2], axis=-1, keepdims=True))
               - jnp.exp(jnp.sum(lq[2:3] * lq[3:4], axis=-1, keepdims=True)) + lam_init)
        k_own = jnp.concatenate([qkv_ref[0, :, w + h * 128:w + (h + 1) * 128] for h in range(heads)], axis=0)
        v_own = jnp.concatenate([qkv_ref[0, :, 2 * w + h * 128:2 * w + (h + 1) * 128] for h in range(heads)],
                                axis=0)
        absorb(k_own, v_own, lax.broadcasted_iota(jnp.int32, (heads * tq, lanes), 0) // tq)
        n = (acc_ref[...] / l_ref[...]).T
        for h in range(heads):
            o = n[2 * h * tq:(2 * h + 1) * tq] - lam * n[(2 * h + 1) * tq:(2 * h + 2) * tq]
            o_ref[0, :, h * 128:(h + 1) * 128] = _rms(o, g_ref[...]) * (1.0 - lam_init)


def _attention_cached(qkv, cache_k, cache_v, layer, lam_p, g, lam_init, tkc):
    b, tq, _ = qkv.shape
    _, _, past, heads, vd = cache_k.shape
    depth = cache_k.shape[0]
    lanes = heads * 2 * tq
    flat = lambda c: c.reshape(depth, b, past * heads, vd)
    cache_spec = pl.BlockSpec((None, None, tkc * heads, vd), lambda bi, t: (layer, bi, t, 0))
    kern = functools.partial(_attn_cached_kernel, tq=tq, heads=heads, lam_init=lam_init)
    return pl.pallas_call(
        kern,
        grid=(b, past // tkc),
        in_specs=[
            pl.BlockSpec((4, ATT_DH), lambda bi, t: (0, 0)),
            pl.BlockSpec((1, ATT_VD), lambda bi, t: (0, 0)),
            pl.BlockSpec((1, tq, qkv.shape[-1]), lambda bi, t: (bi, 0, 0)),
            cache_spec, cache_spec,
        ],
        out_specs=pl.BlockSpec((1, tq, heads * ATT_VD), lambda bi, t: (bi, 0, 0)),
        out_shape=jax.ShapeDtypeStruct((b, tq, heads * ATT_VD), F32),
        scratch_shapes=[pltpu.VMEM((128, lanes), BF16), pltpu.VMEM((1, lanes), F32),
                        pltpu.VMEM((1, lanes), F32), pltpu.VMEM((ATT_VD, lanes), F32)],
        compiler_params=_params("parallel", "arbitrary"),
        name="diff_attn_cached",
    )(lam_p, g.reshape(1, ATT_VD), qkv, flat(cache_k), flat(cache_v))


def _attn_long_kernel(lam_ref, g_ref, q_ref, k_ref, vt_ref, kp_ref, vtp_ref, o_ref,
                      sa_ref, sb_ref, ta_ref, tb_ref, m_ref, acc_ref, *, tq, tk, n_pre_valid, lam_init):
    qi = pl.program_id(2)
    nt = (((1,), (1,)), ((), ()))

    q = q_ref[0]
    r = lax.broadcasted_iota(jnp.int32, (128, 128), 0)
    c = lax.broadcasted_iota(jnp.int32, (128, 128), 1)
    e1 = jnp.where((r == c) & (r < ATT_DH), 1.0, 0.0).astype(BF16)
    e2 = jnp.where((r == c) & (r >= ATT_DH), 1.0, 0.0).astype(BF16)
    q2t = jnp.concatenate([lax.dot_general(e1, q, nt, preferred_element_type=F32),
                           lax.dot_general(e2, q, nt, preferred_element_type=F32)], axis=1).astype(BF16)

    m_ref[...] = jnp.full(m_ref.shape, -jnp.inf, F32)
    acc_ref[...] = jnp.zeros(acc_ref.shape, F32)

    def scores(kt, mask=None):
        s = jnp.dot(kt, q2t, preferred_element_type=F32)
        return s if mask is None else jnp.where(mask, s, -jnp.inf)

    def scores_to(s_ref, top_ref, t, mask=None):
        off = pl.multiple_of(t * tk, tk)
        s = scores(k_ref[0, pl.ds(off, tk), :], mask)
        s_ref[...] = s
        top_ref[...] = jnp.max(s, axis=0, keepdims=True)

    def absorb(s, top, vt):
        m_old = m_ref[...]
        m_new = jnp.maximum(m_old, top)
        alpha = jnp.exp2(m_old - m_new)
        p = jnp.exp2(s - m_new).astype(BF16)
        m_ref[...] = m_new
        acc_ref[...] = alpha * acc_ref[...] + jnp.dot(vt, p, preferred_element_type=F32)

    def step(cur, nxt, t_next, t, mask=None):
        scores_to(nxt[0], nxt[1], t_next, mask)
        absorb(cur[0][...], cur[1][...], vt_ref[0, t, 0])

    buf_a = (sa_ref, ta_ref)
    buf_b = (sb_ref, tb_ref)

    n_pre = kp_ref.shape[1]
    key = lax.broadcasted_iota(jnp.int32, (n_pre, 2 * tq), 0)
    s_pre = scores(kp_ref[0], key < n_pre_valid)
    absorb(s_pre, jnp.max(s_pre, axis=0, keepdims=True), vtp_ref[0, 0])

    key = lax.broadcasted_iota(jnp.int32, (tk, 2 * tq), 0)
    qry = lax.broadcasted_iota(jnp.int32, (tk, 2 * tq), 1)
    qry = jnp.where(qry >= tq, qry - tq, qry)
    shift = CHUNK.bit_length() - 1
    n_full = 2 * qi
    last = jnp.maximum(n_full - 1, 0)
    scores_to(sa_ref, ta_ref, n_full, (key >> shift) <= (qry >> shift))
    step(buf_a, buf_b, n_full + 1, n_full, ((key + tk) >> shift) <= (qry >> shift))
    step(buf_b, buf_a, 0, n_full + 1)

    def pair(t0):
        step(buf_a, buf_b, t0 + 1, t0)
        step(buf_b, buf_a, jnp.minimum(t0 + 2, last), t0 + 1)

    def quad(u, carry):
        pair(4 * u)
        pair(4 * u + 2)
        return carry
    lax.fori_loop(0, qi // 2, quad, 0)

    @pl.when(qi % 2 == 1)
    def _():
        pair(n_full - 2)

    lq = lam_ref[...]
    lam = (jnp.exp(jnp.sum(lq[0:1] * lq[1:2], axis=-1, keepdims=True))
           - jnp.exp(jnp.sum(lq[2:3] * lq[3:4], axis=-1, keepdims=True)) + lam_init)
    acc = acc_ref[...]
    num = acc[:ATT_VD]
    den = acc[ATT_VD:ATT_VD + 1]
    o_t = num[:, :tq] / den[:, :tq] - lam * (num[:, tq:] / den[:, tq:])
    o_ref[0] = _rms(o_t.T, g_ref[...]) * (1.0 - lam_init)


VT_ROWS = ATT_VD + 16


def _values_t(v):
    ones = jnp.ones(v.shape[:-1] + (1,), v.dtype)
    zeros = jnp.zeros(v.shape[:-1] + (VT_ROWS - ATT_VD - 1,), v.dtype)
    return jnp.swapaxes(jnp.concatenate([v, ones, zeros], axis=-1), -1, -2)


def _attention_long(qkv, pre_k, pre_vt, n_pre_valid, lam_p, g, lam_init, tq, heads):
    b, L, _ = qkv.shape
    tk = tq // 2
    nt = L // tk
    lp = pre_k.shape[1]
    vt = jnp.swapaxes(qkv[:, :, 2 * heads * 128:].reshape(b, nt, tk, heads * ATT_VD), 2, 3)
    vt = vt.reshape(b, nt, heads, ATT_VD, tk)
    extra = jnp.zeros((b, nt, heads, VT_ROWS - ATT_VD, tk), vt.dtype).at[:, :, :, 0].set(1.0)
    vt = jnp.concatenate([vt, extra], axis=3)
    kern = functools.partial(_attn_long_kernel, tq=tq, tk=tk, n_pre_valid=n_pre_valid, lam_init=lam_init)
    return pl.pallas_call(
        kern,
        grid=(b, heads, L // tq),
        in_specs=[
            pl.BlockSpec((4, ATT_DH), lambda bi, h, qi: (0, 0)),
            pl.BlockSpec((1, ATT_VD), lambda bi, h, qi: (0, 0)),
            pl.BlockSpec((1, tq, 128), lambda bi, h, qi: (bi, qi, h)),
            pl.BlockSpec((1, L, 128), lambda bi, h, qi: (bi, 0, heads + h)),
            pl.BlockSpec((1, nt, 1, VT_ROWS, tk), lambda bi, h, qi: (bi, 0, h, 0, 0)),
            pl.BlockSpec((1, lp, 128), lambda bi, h, qi: (bi, 0, h)),
            pl.BlockSpec((1, 1, VT_ROWS, lp), lambda bi, h, qi: (bi, h, 0, 0)),
        ],
        out_specs=pl.BlockSpec((1, tq, 128), lambda bi, h, qi: (bi, qi, h)),
        out_shape=jax.ShapeDtypeStruct((b, L, heads * ATT_VD), F32),
        scratch_shapes=[pltpu.VMEM((tk, 2 * tq), F32), pltpu.VMEM((tk, 2 * tq), F32),
                        pltpu.VMEM((1, 2 * tq), F32), pltpu.VMEM((1, 2 * tq), F32),
                        pltpu.VMEM((1, 2 * tq), F32), pltpu.VMEM((VT_ROWS, 2 * tq), F32)],
        compiler_params=_params("parallel", "parallel", "arbitrary"),
        name="diff_attn_long",
    )(lam_p, g.reshape(1, ATT_VD), qkv, qkv, vt, pre_k, pre_vt)


def _lru_kernel(p_ref, c0_ref, h0_ref, cw_ref, cb_ref, wa_ref, ba_ref, wx_ref, bx_ref, ll_ref,
                o_ref, hout_ref, cout_ref, ext_ref, hc_ref, *, tl, width, conv):
    li = pl.program_id(1)
    pad = 8

    @pl.when(li == 0)
    def _():
        ext_ref[pad - (conv - 1):pad] = c0_ref[0]
        hc_ref[...] = h0_ref[0]

    lx = p_ref[0, :, 0:width]
    lg = p_ref[0, :, width:2 * width]
    ext_ref[pad:pad + tl] = lx
    cw = cw_ref[...]
    xc = cb_ref[...] + cw[conv - 1:conv] * lx
    for s in range(1, conv):
        xc = xc + cw[conv - 1 - s:conv - s] * ext_ref[pad - s:pad - s + tl]
    tail = ext_ref[pad + tl - (conv - 1):pad + tl]
    cout_ref[0] = tail
    ext_ref[pad - (conv - 1):pad] = tail

    xb = xc.astype(BF16)
    r = jax.nn.sigmoid(jnp.dot(xb, wa_ref[...], preferred_element_type=F32) + ba_ref[...])
    i = jax.nn.sigmoid(jnp.dot(xb, wx_ref[...], preferred_element_type=F32) + bx_ref[...])
    z = -ll_ref[...]
    softplus = jnp.maximum(z, 0.0) + jnp.log1p(jnp.exp(-jnp.abs(z)))
    log_a = -LRU_C * r * softplus
    a = jnp.exp(log_a)
    u = jnp.sqrt(-jnp.tanh(log_a) * (1.0 + a * a)) * i * xc

    row = lax.broadcasted_iota(jnp.int32, a.shape, 0)
    d = 1
    while d < tl:
        keep = row >= d
        a_prev = jnp.where(keep, pltpu.roll(a, d, 0), 1.0)
        u_prev = jnp.where(keep, pltpu.roll(u, d, 0), 0.0)
        u = u + a * u_prev
        a = a * a_prev
        d *= 2
    h = u + a * hc_ref[...]
    hc_ref[...] = h[tl - 1:tl]
    hout_ref[0] = h[tl - 1:tl]
    o_ref[0] = h * jax.nn.gelu(lg)


def _lru(proj, c0, h0, cw, cb, wa, ba, wx, bx, ll, tl):
    b, L, _ = proj.shape
    conv, width = cw.shape
    vec = lambda a: a.reshape(1, width)
    const = lambda shape: pl.BlockSpec(shape, lambda bi, li: (0,) * len(shape))
    kern = functools.partial(_lru_kernel, tl=tl, width=width, conv=conv)
    return pl.pallas_call(
        kern,
        grid=(b, L // tl),
        in_specs=[
            pl.BlockSpec((1, tl, 2 * width), lambda bi, li: (bi, li, COL_LRU)),
            pl.BlockSpec((1, conv - 1, width), lambda bi, li: (bi, 0, 0)),
            pl.BlockSpec((1, 1, width), lambda bi, li: (bi, 0, 0)),
            const((conv, width)), const((1, width)),
            const((width, width)), const((1, width)),
            const((width, width)), const((1, width)), const((1, width)),
        ],
        out_specs=[
            pl.BlockSpec((1, tl, width), lambda bi, li: (bi, li, 0)),
            pl.BlockSpec((1, 1, width), lambda bi, li: (bi, 0, 0)),
            pl.BlockSpec((1, conv - 1, width), lambda bi, li: (bi, 0, 0)),
        ],
        out_shape=[jax.ShapeDtypeStruct((b, L, width), F32),
                   jax.ShapeDtypeStruct((b, 1, width), F32),
                   jax.ShapeDtypeStruct((b, conv - 1, width), F32)],
        scratch_shapes=[pltpu.VMEM((tl + 8, width), F32), pltpu.VMEM((1, width), F32)],
        compiler_params=_params("parallel", "arbitrary"),
        name="rg_lru",
    )(proj, c0, h0.reshape(b, 1, width), cw, vec(cb), wa, vec(ba), wx, vec(bx), vec(ll))


def _gla_kernel(pa_ref, pb_ref, s0_ref, w2_ref, gb_ref, gn_ref, o_ref, sout_ref, st_ref, oi_ref,
                *, tl, chunk, kw, vw):
    li = pl.program_id(1)

    @pl.when(li == 0)
    def _():
        st_ref[...] = s0_ref[0]

    q = pa_ref[0, :, 0:kw] * (GLA_DK ** -0.5)
    k = pa_ref[0, :, kw:2 * kw]
    v = pa_ref[0, :, 2 * kw:2 * kw + vw]
    gog = pb_ref[0, :, 0:vw]
    glr = pb_ref[0, :, vw:2 * vw]

    x = jnp.dot(glr.astype(BF16), w2_ref[...], preferred_element_type=F32) + gb_ref[...]
    log_g = (jnp.minimum(x, 0.0) - jnp.log1p(jnp.exp(-jnp.abs(x)))) / GLA_TAU

    rowc = lax.broadcasted_iota(jnp.int32, (tl, kw), 0) % chunk
    bc = log_g
    d = 1
    while d < chunk:
        bc = bc + jnp.where(rowc >= d, pltpu.roll(bc, d, 0), 0.0)
        d *= 2

    hk = lax.broadcasted_iota(jnp.int32, (kw, vw), 0) // GLA_DK
    hv = lax.broadcasted_iota(jnp.int32, (kw, vw), 1) // GLA_DV
    spread = jnp.where(hk == hv, 1.0, 0.0).astype(BF16)

    sub = min(GLA_SUB, chunk)
    row_s = rowc % sub
    o_band = jnp.dot((q * k).astype(BF16), spread, preferred_element_type=F32) * v
    for dl in range(1, sub):
        valid = row_s >= dl
        rel = jnp.where(valid, bc - pltpu.roll(bc, dl, 0), -jnp.inf)
        w = q * pltpu.roll(k, dl, 0) * jnp.exp(rel)
        att = jnp.dot(w.astype(BF16), spread, preferred_element_type=F32)
        o_band = o_band + att * pltpu.roll(v, dl, 0)
    oi_ref[...] = o_band

    n_sub = chunk // sub
    nt_dims = (((1,), (1,)), ((), ()))
    hs = lax.broadcasted_iota(jnp.int32, (vw, kw), 0) // GLA_DV
    hl = lax.broadcasted_iota(jnp.int32, (vw, kw), 1) // GLA_DK
    same_head = hs == hl
    heads = kw // GLA_DK
    if n_sub > 1:
        cat = (n_sub - 1) * kw
        kr = lax.broadcasted_iota(jnp.int32, (heads * chunk, cat), 0) // chunk
        kl = (lax.broadcasted_iota(jnp.int32, (heads * chunk, cat), 1) % kw) // GLA_DK
        key_head = kr == kl
        vr = lax.broadcasted_iota(jnp.int32, (heads * chunk, vw), 0) // chunk
        vl = lax.broadcasted_iota(jnp.int32, (heads * chunk, vw), 1) // GLA_DV
        val_head = vr == vl
        ri = lax.broadcasted_iota(jnp.int32, (chunk, kw), 0)
    for c in range(tl // chunk):
        rows = slice(c * chunk, (c + 1) * chunk)
        b_c = bc[rows]
        q_c = q[rows]
        k_c = k[rows]
        v_c = v[rows].astype(BF16)
        o_c = jnp.zeros((chunk, vw), F32)
        if n_sub > 1:
            qs, ks = [], []
            for i in range(1, n_sub):
                r_i = b_c[i * sub - 1:i * sub]
                in_q = (ri >= i * sub) & (ri < (i + 1) * sub)
                in_k = ri < i * sub
                qs.append(jnp.where(in_q, q_c * jnp.exp(jnp.where(in_q, b_c - r_i, 0.0)), 0.0))
                ks.append(jnp.where(in_k, k_c * jnp.exp(jnp.where(in_k, r_i - b_c, 0.0)), 0.0))
            q_cat = jnp.concatenate(qs, axis=1).astype(BF16)
            k_cat = jnp.concatenate(ks, axis=1).astype(BF16)
            k_bd = jnp.where(key_head, jnp.concatenate([k_cat] * heads, axis=0), jnp.zeros_like(k_cat[:1]))
            att = lax.dot_general(q_cat, k_bd, nt_dims, preferred_element_type=F32)
            v_bd = jnp.where(val_head, jnp.concatenate([v_c] * heads, axis=0), jnp.zeros_like(v_c[:1]))
            o_c = jnp.dot(att.astype(BF16), v_bd, preferred_element_type=F32)
        b_last = b_c[chunk - 1:chunk]
        st = st_ref[...]
        qe = (q_c * jnp.exp(b_c)).astype(BF16)
        o_c = o_c + lax.dot_general(qe, st.astype(BF16), nt_dims, preferred_element_type=F32)
        ke = (k_c * jnp.exp(b_last - b_c)).astype(BF16)
        kv_t = lax.dot_general(v_c, ke, (((0,), (0,)), ((), ())), preferred_element_type=F32)
        st_ref[...] = jnp.exp(b_last) * st + jnp.where(same_head, kv_t, 0.0)
        oi_ref[rows, :] += o_c
    sout_ref[0] = st_ref[...]

    o = oi_ref[...]
    o2 = o * o
    lane_h = lax.broadcasted_iota(jnp.int32, o.shape, 1) // GLA_DV
    ms = jnp.zeros_like(o)
    for h in range(vw // GLA_DV):
        in_h = lane_h == h
        s_h = jnp.sum(jnp.where(in_h, o2, 0.0), axis=-1, keepdims=True) * (1.0 / GLA_DV)
        ms = jnp.where(in_h, s_h, ms)
    o_ref[0] = o * lax.rsqrt(ms + EPS) * gn_ref[...] * jax.nn.silu(gog)


def _gla(proj, s0_t, w2p, gb, gn_t, tl):
    b, L, _ = proj.shape
    vw, kw = s0_t.shape[1:]
    chunk = min(CHUNK, L)
    const = lambda shape: pl.BlockSpec(shape, lambda bi, li: (0,) * len(shape))
    kern = functools.partial(_gla_kernel, tl=tl, chunk=chunk, kw=kw, vw=vw)
    return pl.pallas_call(
        kern,
        grid=(b, L // tl),
        in_specs=[
            pl.BlockSpec((1, tl, PROJ_TN), lambda bi, li: (bi, li, COL_GLA_A)),
            pl.BlockSpec((1, tl, PROJ_TN), lambda bi, li: (bi, li, COL_GLA_B)),
            pl.BlockSpec((1, vw, kw), lambda bi, li: (bi, 0, 0)),
            const((vw, kw)), const((1, kw)), const((1, vw)),
        ],
        out_specs=[
            pl.BlockSpec((1, tl, vw), lambda bi, li: (bi, li, 0)),
            pl.BlockSpec((1, vw, kw), lambda bi, li: (bi, 0, 0)),
        ],
        out_shape=[jax.ShapeDtypeStruct((b, L, vw), F32), jax.ShapeDtypeStruct((b, vw, kw), F32)],
        scratch_shapes=[pltpu.VMEM((vw, kw), F32), pltpu.VMEM((tl, vw), F32)],
        compiler_params=_params("parallel", "arbitrary"),
        name="gla",
    )(proj, proj, s0_t, w2p, gb.reshape(1, kw), gn_t.reshape(1, vw))


def _ffn_kernel(x_ref, oa_ref, ol_ref, og_ref, wo_ref, g_ref, c0_ref, wu_ref, wg_ref, cw_ref, cb_ref,
                wd_ref, gf_ref, out_ref, cout_ref, acc_ref, hn_ref, carry_ref,
                *, bb, tl, tn, conv, nsplit, final_norm):
    li = pl.program_id(1)
    j = pl.program_id(2)
    m = bb * tl
    d = x_ref.shape[-1]
    wa = oa_ref.shape[-1]
    wl = ol_ref.shape[-1]
    acc = out_ref.at[0] if bb == 1 else acc_ref

    @pl.when(j == 0)
    def _():
        mix = jnp.dot(oa_ref[...].reshape(m, wa).astype(BF16), wo_ref[0:wa], preferred_element_type=F32)
        mix += jnp.dot(ol_ref[...].reshape(m, wl).astype(BF16), wo_ref[wa:wa + wl],
                       preferred_element_type=F32)
        mix += jnp.dot(og_ref[...].reshape(m, d - wa - wl).astype(BF16), wo_ref[wa + wl:d],
                       preferred_element_type=F32)
        xn = x_ref[...].reshape(m, d) + mix
        acc[...] = xn
        hn_ref[...] = _rms(xn, g_ref[...]).astype(BF16)

    @pl.when(li == 0)
    def _():
        carry_ref[j] = c0_ref[...]

    prev = carry_ref[j]
    ts = tl // nsplit
    t = lax.broadcasted_iota(jnp.int32, (bb, ts, tn), 1)
    cw = cw_ref[...]
    def up_gate(p):
        hn = hn_ref[p * ts * bb:(p + 1) * ts * bb]
        return (jnp.dot(hn, wu_ref[...], preferred_element_type=F32),
                jnp.dot(hn, wg_ref[...], preferred_element_type=F32))

    ahead = up_gate(0)
    for p in range(nsplit):
        rows = slice(p * ts * bb, (p + 1) * ts * bb)
        u, gate = ahead
        if p + 1 < nsplit:
            ahead = up_gate(p + 1)
        u3 = u.reshape(bb, ts, tn)
        uc = cb_ref[...] + cw[conv - 1:conv] * u3
        for s in range(1, conv):
            shifted = pltpu.roll(u, s, 0).reshape(bb, ts, tn)
            for r in range(s):
                shifted = jnp.where(t == r, prev[:, conv - 1 - s + r:conv - s + r, :], shifted)
            uc = uc + cw[conv - 1 - s:conv - s] * shifted
        prev = u3[:, ts - (conv - 1):ts, :]
        act = (jax.nn.gelu(uc) * gate.reshape(bb, ts, tn)).reshape(bb * ts, tn).astype(BF16)
        acc[rows] += jnp.dot(act, wd_ref[...], preferred_element_type=F32)
    carry_ref[j] = prev
    cout_ref[:, j] = prev

    @pl.when(j == pl.num_programs(2) - 1)
    def _():
        if bb == 1:
            if final_norm:
                acc[...] = _rms(acc[...], gf_ref[...])
        else:
            y = acc_ref[...]
            if final_norm:
                y = _rms(y, gf_ref[...])
            out_ref[...] = y.reshape(bb, tl, d)


def _ffn(x, oa, ol, og, wo, g, c0, wu, wg, cw, cb, wd, gf, bb, tl, tn, final_norm):
    b, L, d = x.shape
    conv, f = cw.shape
    nj = f // tn
    nsplit = tl // FFN_PIECE if bb == 1 and tl % FFN_PIECE == 0 else 1
    kern = functools.partial(_ffn_kernel, bb=bb, tl=tl, tn=tn, conv=conv, nsplit=nsplit,
                             final_norm=final_norm)
    act_spec = lambda w: pl.BlockSpec((bb, tl, w), lambda bi, li, j: (bi, li, 0))
    const = lambda shape: pl.BlockSpec(shape, lambda bi, li, j: (0,) * len(shape))
    return pl.pallas_call(
        kern,
        grid=(b // bb, L // tl, nj),
        in_specs=[
            act_spec(d), act_spec(oa.shape[-1]), act_spec(ol.shape[-1]), act_spec(og.shape[-1]),
            const((d, d)), const((1, d)),
            pl.BlockSpec((bb, conv - 1, tn), lambda bi, li, j: (bi, 0, j)),
            pl.BlockSpec((d, tn), lambda bi, li, j: (0, j)),
            pl.BlockSpec((d, tn), lambda bi, li, j: (0, j)),
            pl.BlockSpec((conv, tn), lambda bi, li, j: (0, j)),
            pl.BlockSpec((1, tn), lambda bi, li, j: (0, j)),
            pl.BlockSpec((tn, d), lambda bi, li, j: (j, 0)),
            const((1, d)),
        ],
        out_specs=[
            pl.BlockSpec((bb, tl, d), lambda bi, li, j: (bi, li, 0)),
            pl.BlockSpec((bb, nj, conv - 1, tn), lambda bi, li, j: (bi, 0, 0, 0)),
        ],
        out_shape=[jax.ShapeDtypeStruct((b, L, d), F32), jax.ShapeDtypeStruct((b, nj, conv - 1, tn), F32)],
        scratch_shapes=[pltpu.VMEM((8, 128) if bb == 1 else (bb * tl, d), F32),
                        pltpu.VMEM((bb * tl, d), BF16), pltpu.VMEM((nj, bb, conv - 1, tn), F32)],
        compiler_params=pltpu.CompilerParams(dimension_semantics=("parallel", "arbitrary", "arbitrary"),
                                             vmem_limit_bytes=FFN_VMEM_LIMIT),
        name="mix_ffn",
    )(x, oa, ol, og, wo, g.reshape(1, d), c0, wu, wg, cw, cb.reshape(1, f), wd, gf.reshape(1, d))


def _block_diag(w):
    n, c, dd = w.shape
    eye = jnp.eye(n, dtype=w.dtype)
    return jnp.einsum('ncd,nm->ncmd', w, eye).reshape(n * c, n * dd)


def _prep_layer(l, w_in, lru_gate_a_w, lru_gate_x_w, gla_gate_w2, gla_norm_g, w_out, ffn_w_up,
                ffn_w_gate, ffn_w_down, sizes):
    att_w, lru_w, gla_kw, gla_vw = sizes
    o = 0
    parts = {}
    for name, wdt in (("q", att_w), ("k", att_w), ("v", att_w), ("lx", lru_w), ("lg", lru_w),
                      ("gq", gla_kw), ("gk", gla_kw), ("gv", gla_vw), ("glr", GLA_RANK), ("gog", gla_vw)):
        parts[name] = w_in[l][:, o:o + wdt]
        o += wdt
    d = w_in.shape[1]
    pad = jnp.zeros((d, PROJ_W - o), w_in.dtype)
    order = ("q", "k", "v", "lx", "lg", "gq", "gk", "gv", "gog", "glr")
    w_in_p = jnp.concatenate([parts[n] for n in order] + [pad], axis=1).astype(BF16)
    w2p = jnp.zeros((gla_vw, gla_kw), F32).at[:GLA_RANK].set(gla_gate_w2[l]).astype(BF16)
    return dict(
        w_in=w_in_p,
        wa=_block_diag(lru_gate_a_w[l]).astype(BF16),
        wx=_block_diag(lru_gate_x_w[l]).astype(BF16),
        w2p=w2p,
        gn_t=jnp.tile(gla_norm_g[l], gla_vw // GLA_DV),
        w_out=w_out[l].astype(BF16),
        wu=ffn_w_up[l].astype(BF16), wg=ffn_w_gate[l].astype(BF16), wd=ffn_w_down[l].astype(BF16),
    )


def _state_to_bd(s):
    b, h, dk, dv = s.shape
    eye = jnp.eye(h, dtype=s.dtype)
    return jnp.einsum('bhde,hg->bhegd', s, eye).reshape(b, h * dv, h * dk)


def _state_from_bd(st, h):
    b, vw, kw = st.shape
    dv, dk = vw // h, kw // h
    s5 = st.reshape(b, h, dv, h, dk)
    diag = jnp.stack([s5[:, i, :, i, :] for i in range(h)], axis=1)
    return diag.transpose(0, 1, 3, 2)


def _pick_tile(n, target):
    t = min(n, target)
    while n % t:
        t //= 2
    return t


def _trunk(x, states, pre_kv, n_pre_valid, layers, prm, heads, gla_heads, tiles, long_attn=False,
           cache=None):
    b, L, d = x.shape
    depth = len(layers)
    tm, tq, tl_lru, tl_gla, bb, tl_ffn, tn = tiles
    outs = []
    for l in range(depth):
        lw = layers[l]
        width = prm['lru_conv_w'].shape[-1]
        f = prm['ffn_conv_w'].shape[-1]
        if states is None:
            h0 = jnp.zeros((b, width), F32)
            c0 = jnp.zeros((b, prm['lru_conv_w'].shape[1] - 1, width), F32)
            s0 = jnp.zeros((b, gla_heads, GLA_DK, GLA_DV), F32)
            fc0 = jnp.zeros((b, prm['ffn_conv_w'].shape[1] - 1, f), F32)
        else:
            h0, c0, s0, fc0 = states[l]
        k32, v32, pf, pb = _in_proj(x.reshape(b * L, d), prm['norm_mix_g'][l], lw['w_in'], tm)
        pf = pf.reshape(b, L, MIX_W)
        pb = pb.reshape(b, L, QKV_W)
        lam_init = 0.8 - 0.6 * math.exp(-0.3 * l)
        lam_p, sub_g = prm['attn_lambda'][l], prm['attn_subln_g'][l]
        if cache is not None:
            o_att = _attention_cached(pb, cache[0], cache[1], l, lam_p, sub_g, lam_init,
                                      _pick_tile(cache[0].shape[2], 1024))
        elif long_attn:
            o_att = _attention_long(pb, pre_kv[l][0], pre_kv[l][1], n_pre_valid, lam_p, sub_g, lam_init,
                                    tq, heads)
        else:
            o_att = _attention(pb, None, None, 0, lam_p, sub_g, lam_init, tq, heads)
        o_lru, h1, c1 = _lru(pf, c0, h0, prm['lru_conv_w'][l], prm['lru_conv_b'][l], lw['wa'],
                             prm['lru_gate_a_b'][l], lw['wx'], prm['lru_gate_x_b'][l],
                             prm['lru_log_lambda'][l], tl_lru)
        o_gla, s1_t = _gla(pf, _state_to_bd(s0), lw['w2p'], prm['gla_gate_b'][l], lw['gn_t'], tl_gla)
        x, fc1 = _ffn(x, o_att, o_lru, o_gla, lw['w_out'], prm['norm_ffn_g'][l], fc0, lw['wu'], lw['wg'],
                      prm['ffn_conv_w'][l], prm['ffn_conv_b'][l], lw['wd'], prm['norm_final_g'],
                      bb, tl_ffn, tn, final_norm=(l == depth - 1))
        fc1 = fc1.transpose(0, 2, 1, 3).reshape(b, fc1.shape[2], f)
        aw = heads * ATT_VD
        outs.append(dict(k=k32.reshape(b, L, heads, ATT_VD), v=v32.reshape(b, L, heads, ATT_VD),
                         kb=pb[:, :, aw:2 * aw],
                         vb=pb[:, :, 2 * aw:3 * aw], h=h1.reshape(b, width), c=c1,
                         s=_state_from_bd(s1_t, gla_heads), fc=fc1))
    return x, outs


def kernel(x_prompt, x_sample, cache_attn_k, cache_attn_v, state_lru_h, state_lru_conv, state_gla, state_ffn_conv, meta_tokens, norm_mix_g, w_in, attn_lambda, attn_subln_g, lru_conv_w, lru_conv_b, lru_gate_a_w, lru_gate_a_b, lru_gate_x_w, lru_gate_x_b, lru_log_lambda, gla_gate_w2, gla_gate_b, gla_norm_g, w_out, norm_ffn_g, ffn_w_up, ffn_conv_w, ffn_conv_b, ffn_w_gate, ffn_w_down, norm_final_g):
    depth = w_in.shape[0]
    b, seq, d = x_prompt.shape
    db, dseq, _ = x_sample.shape
    past = cache_attn_k.shape[2]
    heads = cache_attn_k.shape[3]
    gla_heads = state_gla.shape[2]
    n_meta = meta_tokens.shape[0]
    width = lru_conv_w.shape[-1]
    f = ffn_conv_w.shape[-1]
    att_w = heads * ATT_VD
    sizes = (att_w, width, gla_heads * GLA_DK, gla_heads * GLA_DV)

    prm = dict(norm_mix_g=norm_mix_g, attn_lambda=attn_lambda, attn_subln_g=attn_subln_g,
               lru_conv_w=lru_conv_w, lru_conv_b=lru_conv_b, lru_gate_a_b=lru_gate_a_b,
               lru_gate_x_b=lru_gate_x_b, lru_log_lambda=lru_log_lambda, gla_gate_b=gla_gate_b,
               norm_ffn_g=norm_ffn_g, ffn_conv_w=ffn_conv_w, ffn_conv_b=ffn_conv_b,
               norm_final_g=norm_final_g)
    layers = [_prep_layer(l, w_in, lru_gate_a_w, lru_gate_x_w, gla_gate_w2, gla_norm_g, w_out,
                          ffn_w_up, ffn_w_gate, ffn_w_down, sizes) for l in range(depth)]
    tn = 256 if f % 256 == 0 else 128

    xm = jnp.broadcast_to(meta_tokens.astype(F32)[None], (b, n_meta, d))
    tiles_m = (b * n_meta, n_meta, n_meta, n_meta, b, n_meta, tn)
    _, om = _trunk(xm, None, None, 0, layers, prm, heads, gla_heads, tiles_m)

    pad_rows = 128 - n_meta
    pre_f = [(jnp.pad(o['kb'], ((0, 0), (0, pad_rows), (0, 0))),
              _values_t(jnp.pad(o['vb'], ((0, 0), (0, pad_rows), (0, 0)))
                        .reshape(b, 128, heads, ATT_VD).transpose(0, 2, 1, 3))) for o in om]
    st_f = [(o['h'], o['c'], o['s'], o['fc']) for o in om]
    tiles_f = (_pick_tile(b * seq, 1024), _pick_tile(seq, 1024), _pick_tile(seq, 512),
               _pick_tile(seq, 256), 1, _pick_tile(seq, 1024), f // 2)
    yp, of = _trunk(x_prompt, st_f, pre_f, n_meta, layers, prm, heads, gla_heads, tiles_f,
                    long_attn=True)

    st_s = [(state_lru_h[l], state_lru_conv[l], state_gla[l], state_ffn_conv[l]) for l in range(depth)]
    tiles_s = (_pick_tile(db * dseq, 1024), dseq, dseq, dseq, db, dseq, tn)
    ys, os_ = _trunk(x_sample, st_s, None, past, layers, prm, heads, gla_heads, tiles_s,
                     cache=(cache_attn_k, cache_attn_v))

    def stack_p(name, tail_shape):
        return jnp.stack([jnp.concatenate([m[name].reshape((b, n_meta) + tail_shape),
                                           fr[name].reshape((b, seq) + tail_shape)], axis=1)
                          for m, fr in zip(om, of)])

    k_p = stack_p('k', (heads, ATT_VD))
    v_p = stack_p('v', (heads, ATT_VD))
    k_s = jnp.stack([o['k'].reshape(db, dseq, heads, ATT_VD) for o in os_])
    v_s = jnp.stack([o['v'].reshape(db, dseq, heads, ATT_VD) for o in os_])
    st = lambda outs, name: jnp.stack([o[name] for o in outs])
    return (yp, ys, k_p, v_p, st(of, 'h'), st(of, 'c'), st(of, 's'), st(of, 'fc'),
            k_s, v_s, st(os_, 'h'), st(os_, 'c'), st(os_, 's'), st(os_, 'fc'))
```
